```python
import jax, jax.numpy as jnp
from jax import lax
import numpy as np

D_MODEL = 2048
BATCH = 2
SEQ = 16384
DEPTH = 2

N_META = 16
GRID_W = 64
EPS = 1e-6
NA_HEADS = 8
NA_DIM = 128
NA_WIDTH = NA_HEADS * NA_DIM
WIN_H = 8
WIN_W = 16
NA_QB = 16
NA_KSPAN = NA_QB + WIN_W
ML_HEADS = 4
ML_QK = 128
ML_V = 256
ML_QK_WIDTH = ML_HEADS * ML_QK
ML_WIDTH = ML_HEADS * ML_V
ML_CHUNK = 64
MIX_WIDTH = NA_WIDTH + ML_WIDTH
EVEN_SIZES = (NA_WIDTH, NA_WIDTH, NA_WIDTH, ML_QK_WIDTH, ML_QK_WIDTH, ML_WIDTH, ML_WIDTH, 4 * ML_HEADS)
EVEN_IN = 3 * NA_WIDTH + 2 * ML_QK_WIDTH + 2 * ML_WIDTH + 4 * ML_HEADS
GLA_HEADS = 4
GLA_DK = 256
GLA_DV = 512
GLA_KW = GLA_HEADS * GLA_DK
GLA_VW = GLA_HEADS * GLA_DV
GLA_RANK = 16
GLA_TAU = 16.0
GLA_CHUNK = 64
ODD_SIZES = (GLA_KW, GLA_KW, GLA_VW, GLA_VW, GLA_RANK, GLA_RANK)
ODD_IN = 2 * GLA_KW + 2 * GLA_VW + 2 * GLA_RANK
D_FF = 5632
CONV_W = 3
N_EVEN = (DEPTH + 1) // 2
N_ODD = DEPTH // 2

kernel_name = 'hybrid_natten_mlstm_gla_encoder'


def _rms(x, gain):
    xf = x.astype(jnp.float32)
    y = xf * lax.rsqrt(jnp.mean(xf * xf, axis=-1, keepdims=True) + EPS)
    return (y * gain.astype(jnp.float32)).astype(x.dtype)


def _head_rms(y, n_heads, gain):
    b, t, w = y.shape
    yh = y.reshape(b, t, n_heads, w // n_heads)
    return _rms(yh, gain.reshape(n_heads, w // n_heads)).reshape(b, t, w)


def _split(a, sizes):
    return jnp.split(a, np.cumsum(sizes)[:-1].tolist(), axis=-1)


def _na_columns():
    j = np.arange(GRID_W // NA_QB)
    starts = np.clip(j * NA_QB - WIN_W // 2, 0, GRID_W - NA_KSPAN)
    col_idx = starts[:, None] + np.arange(NA_KSPAN)
    q_col = j[:, None] * NA_QB + np.arange(NA_QB)
    c0 = np.clip(q_col - WIN_W // 2, 0, GRID_W - WIN_W)[..., None]
    kc = col_idx[:, None, :]
    mask = (kc >= c0) & (kc < c0 + WIN_W)
    dc_idx = np.clip(kc - q_col[..., None] + WIN_W - 1, 0, 2 * WIN_W - 2)
    return col_idx, mask, dc_idx


def _neighbourhood_attention(q, k, v, rel_bias):
    b, t, h, d = q.shape
    n_real = t - N_META
    rows = n_real // GRID_W
    win_h = min(WIN_H, rows)
    n_cb = GRID_W // NA_QB
    scale = d ** -0.5
    qm, km, vm = q[:, :N_META], k[:, :N_META], v[:, :N_META]
    qg = q[:, N_META:].reshape(b, rows, GRID_W, h, d)
    kg = k[:, N_META:].reshape(b, rows, GRID_W, h, d)
    vg = v[:, N_META:].reshape(b, rows, GRID_W, h, d)
    s_mm = jnp.einsum('bqhd,bkhd->bhqk', qm, km) * scale
    y_meta = jnp.einsum('bhqk,bkhd->bqhd', jax.nn.softmax(s_mm, axis=-1), vm)
    col_idx, col_mask, dc_idx = _na_columns()
    n_loc = win_h * NA_KSPAN

    def one_row(r):
        r0 = jnp.clip(r - WIN_H // 2, 0, rows - win_h)
        kr = lax.dynamic_slice_in_dim(kg, r0, win_h, axis=1)
        vr = lax.dynamic_slice_in_dim(vg, r0, win_h, axis=1)
        kb = kr[:, :, col_idx]
        vb = vr[:, :, col_idx]
        qb = lax.dynamic_index_in_dim(qg, r, axis=1, keepdims=False).reshape(b, n_cb, NA_QB, h, d)
        s_loc = jnp.einsum('bjqhd,brjkhd->bhjqrk', qb, kb) * scale
        dr_idx = r0 - r + jnp.arange(win_h) + WIN_H - 1
        bias = jnp.transpose(rel_bias[:, dr_idx][:, :, dc_idx], (0, 2, 3, 1, 4))
        s_loc = jnp.where(col_mask[:, :, None, :], s_loc + bias, -jnp.inf).reshape(b, h, n_cb, NA_QB, n_loc)
        s_met = jnp.einsum('bjqhd,bmhd->bhjqm', qb, km) * scale
        p = jax.nn.softmax(jnp.concatenate([s_loc, s_met], axis=-1), axis=-1)
        p_loc = p[..., :n_loc].reshape(b, h, n_cb, NA_QB, win_h, NA_KSPAN)
        y = (jnp.einsum('bhjqrk,brjkhd->bjqhd', p_loc, vb)
             + jnp.einsum('bhjqm,bmhd->bjqhd', p[..., n_loc:], vm))
        return y.reshape(b, GRID_W, h, d)

    y_real = lax.map(one_row, jnp.arange(rows))
    y_real = jnp.moveaxis(y_real, 0, 1).reshape(b, n_real, h, d)
    return jnp.concatenate([y_meta, y_real], axis=1)


def _scan_chunks(step, state, xs, chunk):
    def split(a):
        bb, hh, tt = a.shape[:3]
        return jnp.moveaxis(a.reshape(bb, hh, tt // chunk, chunk, *a.shape[3:]), 2, 0)
    state, ys = lax.scan(step, state, tuple(split(a) for a in xs))
    n, bb, hh, c, e = ys.shape
    return state, jnp.moveaxis(ys, 0, 2).reshape(bb, hh, n * c, e)


def _bidirectional(step, init, xs_fwd, xs_bwd, chunk):
    meta = lambda xs: tuple(a[:, :, :N_META] for a in xs)
    real = lambda xs: tuple(a[:, :, N_META:] for a in xs)
    flip = lambda xs: tuple(jnp.flip(a, axis=2) for a in xs)
    s, y_meta_f = _scan_chunks(step, init, meta(xs_fwd), N_META)
    _, y_real_f = _scan_chunks(step, s, real(xs_fwd), chunk)
    s, y_real_b = _scan_chunks(step, init, flip(real(xs_bwd)), chunk)
    _, y_meta_b = _scan_chunks(step, s, flip(meta(xs_bwd)), N_META)
    y_f = jnp.concatenate([y_meta_f, y_real_f], axis=2)
    y_b = jnp.flip(jnp.concatenate([y_real_b, y_meta_b], axis=2), axis=2)
    return y_f + y_b


def _mlstm_chunk(state, inp):
    c_mat, n_vec, m = state
    q, k, v, ig, log_f = inp
    c = q.shape[2]
    tri = jnp.tril(jnp.ones((c, c), dtype=bool))
    bcum = jnp.cumsum(log_f, axis=-1)
    d_log = jnp.where(tri, bcum[..., :, None] - bcum[..., None, :] + ig[..., None, :], -jnp.inf)
    m_prev = bcum + m[..., None]
    m_t = jnp.maximum(m_prev, jnp.max(d_log, axis=-1))
    w_prev = jnp.exp(m_prev - m_t)
    p = jnp.exp(d_log - m_t[..., None]) * jnp.einsum('bhtd,bhsd->bhts', q, k)
    num = w_prev[..., None] * jnp.einsum('bhtd,bhde->bhte', q, c_mat) + jnp.einsum('bhts,bhse->bhte', p, v)
    den = w_prev * jnp.einsum('bhtd,bhd->bht', q, n_vec) + jnp.sum(p, axis=-1)
    h = num / jnp.maximum(jnp.abs(den), jnp.exp(-m_t))[..., None]
    g = bcum[..., -1:] - bcum + ig
    m_new = jnp.maximum(bcum[..., -1] + m, jnp.max(g, axis=-1))
    w_state = jnp.exp(bcum[..., -1] + m - m_new)
    w_tok = jnp.exp(g - m_new[..., None])
    c_new = w_state[..., None, None] * c_mat + jnp.einsum('bhs,bhsd,bhse->bhde', w_tok, k, v)
    n_new = w_state[..., None] * n_vec + jnp.einsum('bhs,bhsd->bhd', w_tok, k)
    return (c_new, n_new, m_new), h


def _gla_chunk(s_mat, inp):
    q, k, v, log_a = inp
    c = q.shape[2]
    tri = jnp.tril(jnp.ones((c, c), dtype=bool))
    bcum = jnp.cumsum(log_a, axis=2)
    decay = jnp.exp(jnp.where(tri[:, :, None], bcum[:, :, :, None, :] - bcum[:, :, None, :, :], -jnp.inf))
    a = jnp.einsum('bhtd,bhsd,bhtsd->bhts', q, k, decay)
    o = jnp.einsum('bhtd,bhde->bhte', q * jnp.exp(bcum), s_mat) + jnp.einsum('bhts,bhse->bhte', a, v)
    b_end = bcum[:, :, -1]
    s_new = (jnp.exp(b_end)[..., None] * s_mat
             + jnp.einsum('bhsd,bhse->bhde', k * jnp.exp(b_end[:, :, None] - bcum), v))
    return s_new, o


def _even_mixer(h, w_in, gate_bias, q_gain, k_gain, rel_bias, ml_gain, w_out):
    b, t, _ = h.shape
    f32 = jnp.float32
    q_na, k_na, v_na, q_ml, k_ml, v_ml, o_ml, g_ml = _split(h @ w_in, EVEN_SIZES)
    split_heads = lambda a, n: a.reshape(b, t, n, -1).astype(f32)
    y_na = _neighbourhood_attention(_rms(split_heads(q_na, NA_HEADS), q_gain),
                                    _rms(split_heads(k_na, NA_HEADS), k_gain),
                                    split_heads(v_na, NA_HEADS),
                                    rel_bias.astype(f32)).reshape(b, t, NA_WIDTH)
    to_bhtd = lambda a: split_heads(a, ML_HEADS).transpose(0, 2, 1, 3)
    gates = (g_ml.astype(f32) + gate_bias.astype(f32)).reshape(b, t, 4, ML_HEADS).transpose(2, 0, 3, 1)
    ig_f, fg_f, ig_b, fg_b = gates[0], gates[1], gates[2], gates[3]
    qh, kh, vh = to_bhtd(q_ml), to_bhtd(k_ml) * ML_QK ** -0.5, to_bhtd(v_ml)
    init = (jnp.zeros((b, ML_HEADS, ML_QK, ML_V), f32), jnp.zeros((b, ML_HEADS, ML_QK), f32),
            jnp.zeros((b, ML_HEADS), f32))
    h_ml = _bidirectional(_mlstm_chunk, init,
                          (qh, kh, vh, ig_f, jax.nn.log_sigmoid(fg_f)),
                          (qh, kh, vh, ig_b, jax.nn.log_sigmoid(fg_b)), ML_CHUNK)
    h_ml = h_ml.transpose(0, 2, 1, 3).reshape(b, t, ML_WIDTH)
    y_ml = _head_rms(h_ml, ML_HEADS, ml_gain) * jax.nn.sigmoid(o_ml.astype(f32))
    return jnp.concatenate([y_na, y_ml], axis=-1).astype(h.dtype) @ w_out


def _odd_mixer(h, w_in, gate_up, gate_bias, head_gain, w_out):
    b, t, _ = h.shape
    f32 = jnp.float32
    q, k, v, r, lr_f, lr_b = _split(h @ w_in, ODD_SIZES)
    to_bhtd = lambda a: a.reshape(b, t, GLA_HEADS, -1).astype(f32).transpose(0, 2, 1, 3)
    log_gate = lambda lr, j: jax.nn.log_sigmoid(lr.astype(f32) @ gate_up[j].astype(f32)
                                                + gate_bias[j].astype(f32)) / GLA_TAU
    qh, kh, vh = to_bhtd(q) * GLA_DK ** -0.5, to_bhtd(k), to_bhtd(v)
    init = jnp.zeros((b, GLA_HEADS, GLA_DK, GLA_DV), f32)
    o = _bidirectional(_gla_chunk, init,
                       (qh, kh, vh, to_bhtd(log_gate(lr_f, 0))),
                       (qh, kh, vh, to_bhtd(log_gate(lr_b, 1))), GLA_CHUNK)
    o = o.transpose(0, 2, 1, 3).reshape(b, t, GLA_VW)
    y = _head_rms(o, GLA_HEADS, head_gain) * jax.nn.silu(r.astype(f32))
    return y.astype(h.dtype) @ w_out


def _conv_ffn(h, w_up, conv_w, conv_b, w_down):
    u = h @ w_up
    t = u.shape[1]
    pad = CONV_W // 2
    up = jnp.pad(u, ((0, 0), (pad, CONV_W - 1 - pad), (0, 0)))
    acc = conv_b
    for i in range(CONV_W):
        acc = acc + up[:, i:i + t] * conv_w[i]
    g, val = jnp.split(acc, 2, axis=-1)
    return (jax.nn.silu(g) * val) @ w_down


def setup_inputs(seed: int = 0) -> dict:
    key = jax.random.key(seed)
    ks = jax.random.split(key, 24)
    nrm = lambda kk, shape, scale: jax.random.normal(kk, shape, jnp.float32) * scale
    gain = lambda kk, shape: 1.0 + nrm(kk, shape, 0.02)
    f_bias = jnp.linspace(3.0, 6.0, ML_HEADS, dtype=jnp.float32)
    zeros_h = jnp.zeros((ML_HEADS,), jnp.float32)
    gate_offset = jnp.concatenate([zeros_h, f_bias, zeros_h, f_bias])
    return {
        'x': nrm(ks[0], (BATCH, SEQ, D_MODEL), 1.0),
        'meta_tokens': nrm(ks[1], (N_META, D_MODEL), 1.0),
        'norm_mix': gain(ks[2], (DEPTH, D_MODEL)),
        'norm_ffn': gain(ks[3], (DEPTH, D_MODEL)),
        'ab_w_in': nrm(ks[4], (N_EVEN, D_MODEL, EVEN_IN), D_MODEL ** -0.5),
        'ab_gate_bias': nrm(ks[5], (N_EVEN, 4 * ML_HEADS), 0.01) + gate_offset,
        'ab_q_gain': gain(ks[6], (N_EVEN, NA_DIM)),
        'ab_k_gain': gain(ks[7], (N_EVEN, NA_DIM)),
        'ab_rel_bias': nrm(ks[8], (N_EVEN, NA_HEADS, 2 * WIN_H - 1, 2 * WIN_W - 1), 0.02),
        'ab_ml_gain': gain(ks[9], (N_EVEN, ML_WIDTH)),
        'ab_w_out': nrm(ks[10], (N_EVEN, MIX_WIDTH, D_MODEL), MIX_WIDTH ** -0.5),
        'c_w_in': nrm(ks[11], (N_ODD, D_MODEL, ODD_IN), D_MODEL ** -0.5),
        'c_gate_up': nrm(ks[12], (N_ODD, 2, GLA_RANK, GLA_KW), GLA_RANK ** -0.5),
        'c_gate_bias': nrm(ks[13], (N_ODD, 2, GLA_KW), 0.1),
        'c_head_gain': gain(ks[14], (N_ODD, GLA_VW)),
        'c_w_out': nrm(ks[15], (N_ODD, GLA_VW, D_MODEL), GLA_VW ** -0.5),
        'ffn_w_up': nrm(ks[16], (DEPTH, D_MODEL, 2 * D_FF), D_MODEL ** -0.5),
        'ffn_conv_w': nrm(ks[17], (DEPTH, CONV_W, 2 * D_FF), CONV_W ** -0.5),
        'ffn_conv_b': nrm(ks[18], (DEPTH, 2 * D_FF), 0.01),
        'ffn_w_down': nrm(ks[19], (DEPTH, D_FF, D_MODEL), D_FF ** -0.5),
    }


def reference(x, meta_tokens, norm_mix, norm_ffn, ab_w_in, ab_gate_bias, ab_q_gain, ab_k_gain,
              ab_rel_bias, ab_ml_gain, ab_w_out, c_w_in, c_gate_up, c_gate_bias, c_head_gain,
              c_w_out, ffn_w_up, ffn_conv_w, ffn_conv_b, ffn_w_down):
    b = x.shape[0]
    meta = jnp.broadcast_to(meta_tokens.astype(x.dtype)[None], (b, N_META, x.shape[-1]))
    h = jnp.concatenate([meta, x], axis=1)
    for i in range(DEPTH):
        j = i // 2
        hn = _rms(h, norm_mix[i])
        if i % 2 == 0:
            h = h + _even_mixer(hn, ab_w_in[j], ab_gate_bias[j], ab_q_gain[j], ab_k_gain[j],
                                ab_rel_bias[j], ab_ml_gain[j], ab_w_out[j])
        else:
            h = h + _odd_mixer(hn, c_w_in[j], c_gate_up[j], c_gate_bias[j], c_head_gain[j], c_w_out[j])
        h = h + _conv_ffn(_rms(h, norm_ffn[i]), ffn_w_up[i], ffn_conv_w[i], ffn_conv_b[i], ffn_w_down[i])
    return h[:, N_META:]
```

```python
import functools

import numpy as np
import jax
import jax.numpy as jnp
from jax import lax
from jax.experimental import pallas as pl
from jax.experimental.pallas import tpu as pltpu

F32 = jnp.float32
BF16 = jnp.bfloat16
EPS = 1e-6

N_META = 16
GRID_W = 64
NA_HEADS = 8
NA_DIM = 128
NA_WIDTH = NA_HEADS * NA_DIM
WIN_H = 8
WIN_W = 16
ML_HEADS = 4
ML_QK = 128
ML_V = 256
ML_QK_WIDTH = ML_HEADS * ML_QK
ML_WIDTH = ML_HEADS * ML_V
N_ML_GATES = 4 * ML_HEADS
GLA_HEADS = 4
GLA_DK = 256
GLA_DV = 512
GLA_KW = GLA_HEADS * GLA_DK
GLA_VW = GLA_HEADS * GLA_DV
GLA_RANK = 16
GLA_TAU = 16.0
GLA_CHUNK = 64
CONV_W = 3

LANES = 128
BLK = 256
PAD = BLK - N_META
NA_GROUP = 4
NA_UNION = NA_GROUP + WIN_H - 1
HALO = 16
VMEM_LIMIT = 56 * 1024 * 1024

_NT = (((1,), (1,)), ((), ()))
_TN = (((0,), (0,)), ((), ()))


def _log_sigmoid(x):
    return jnp.minimum(x, 0.0) - jnp.log1p(jnp.exp(-jnp.abs(x)))


def _sigmoid(x):
    return 1.0 / (1.0 + jnp.exp(-x))


def _pick_tile(n, prefs):
    for t in prefs:
        if n % t == 0:
            return t
    raise ValueError(f"no tile in {prefs} divides {n}")


def _params(sem, vmem=VMEM_LIMIT):
    return pltpu.CompilerParams(dimension_semantics=sem, vmem_limit_bytes=vmem)


def _rms_rows(x_ref, gain_ref, dst_ref, dst_row0, n_rows, rc):
    g = gain_ref[...]

    def body(r, carry):
        r0 = pl.multiple_of(r * rc, rc)
        xx = x_ref[pl.ds(r0, rc), :]
        ms = jnp.mean(xx * xx, axis=-1, keepdims=True)
        dst_ref[pl.ds(dst_row0 + r0, rc), :] = (xx * lax.rsqrt(ms + EPS) * g).astype(dst_ref.dtype)
        return carry

    lax.fori_loop(0, n_rows // rc, body, 0)


def _row_valid(tile_idx, tiles_per_batch, tm):
    it = tile_idx % tiles_per_batch
    row = lax.broadcasted_iota(jnp.int32, (tm, 1), 0)
    return (row + it * tm) >= PAD


def _in_proj_kernel(x_ref, gain_ref, w_ref, cs_ref, wt_ref, out_ref, tail_ref, xn_ref, *,
                    n_main, n_rms, tail_nt, rc):
    j = pl.program_id(1)
    tm = x_ref.shape[0]
    tn = w_ref.shape[1]

    @pl.when(j == 0)
    def _():
        _rms_rows(x_ref, gain_ref, xn_ref, 0, tm, rc)

    @pl.when(j < n_main)
    def _():
        acc = jnp.dot(xn_ref[...], w_ref[...], preferred_element_type=F32)
        cs = cs_ref[...]
        if n_rms > 0:
            @pl.when(j < n_rms)
            def _():
                for hh in range(tn // LANES):
                    sl = slice(hh * LANES, (hh + 1) * LANES)
                    a = acc[:, sl]
                    ms = jnp.mean(a * a, axis=-1, keepdims=True)
                    out_ref[:, sl] = (a * lax.rsqrt(ms + EPS) * cs[:, sl]).astype(out_ref.dtype)

            @pl.when(j >= n_rms)
            def _():
                out_ref[...] = (acc * cs).astype(out_ref.dtype)
        else:
            out_ref[...] = (acc * cs).astype(out_ref.dtype)

    @pl.when(j == n_main)
    def _():
        if tail_nt:
            tail_ref[...] = lax.dot_general(wt_ref[...], xn_ref[...], _NT, preferred_element_type=F32)
        else:
            tail_ref[...] = jnp.dot(xn_ref[...], wt_ref[...],
                                    preferred_element_type=F32).astype(tail_ref.dtype)


def _in_proj(h, gain, w_main, colscale, w_tail, *, n_rms, tail_nt, tm, tn):
    n, d = h.shape
    n_main = w_main.shape[1] // tn
    if tail_nt:
        n_tail = w_tail.shape[0]
        wt_spec = pl.BlockSpec((n_tail, d), lambda i, j: (0, 0))
        tail_shape = jax.ShapeDtypeStruct((n_tail, n), F32)
        tail_spec = pl.BlockSpec((n_tail, tm), lambda i, j: (0, i))
    else:
        n_tail = w_tail.shape[1]
        wt_spec = pl.BlockSpec((d, n_tail), lambda i, j: (0, 0))
        tail_shape = jax.ShapeDtypeStruct((n, n_tail), BF16)
        tail_spec = pl.BlockSpec((tm, n_tail), lambda i, j: (i, 0))
    last = n_main - 1
    kern = functools.partial(_in_proj_kernel, n_main=n_main, n_rms=n_rms, tail_nt=tail_nt, rc=128)
    return pl.pallas_call(
        kern,
        grid=(n // tm, n_main + 1),
        in_specs=[
            pl.BlockSpec((tm, d), lambda i, j: (i, 0)),
            pl.BlockSpec((1, d), lambda i, j: (0, 0)),
            pl.BlockSpec((d, tn), lambda i, j: (0, jnp.minimum(j, last))),
            pl.BlockSpec((1, tn), lambda i, j: (0, jnp.minimum(j, last))),
            wt_spec,
        ],
        out_specs=[
            pl.BlockSpec((tm, tn), lambda i, j: (i, jnp.minimum(j, last))),
            tail_spec,
        ],
        out_shape=[jax.ShapeDtypeStruct((n, w_main.shape[1]), BF16), tail_shape],
        scratch_shapes=[pltpu.VMEM((tm, d), BF16)],
        compiler_params=_params(("parallel", "arbitrary")),
    )(h, gain, w_main, colscale, w_tail)


def _mlstm_gates_kernel(graw_ref, bias_ref, gr_ref, gc_ref, *, nb):
    blk = pl.program_id(0) % nb
    g = graw_ref[...] + bias_ref[...]
    row = lax.broadcasted_iota(jnp.int32, g.shape, 0)
    lane = lax.broadcasted_iota(jnp.int32, g.shape, 1)
    is_forget = (row & 4) != 0
    valid = (lane + blk * BLK) >= PAD
    lf = jnp.where(valid, _log_sigmoid(g), 0.0)
    pre = lf
    suf = lf
    d = 1
    while d < BLK:
        pre = pre + jnp.where(lane >= d, pltpu.roll(pre, d, axis=1), 0.0)
        suf = suf + jnp.where(lane < BLK - d, pltpu.roll(suf, BLK - d, axis=1), 0.0)
        d *= 2
    ig = jnp.where(valid, g, -jnp.inf)
    out = jnp.where(is_forget, jnp.where(row >= 8, suf, pre), ig)
    gr_ref[...] = out
    full = jnp.concatenate([out, jnp.zeros((LANES - N_ML_GATES, BLK), F32)], axis=0)
    gc_ref[...] = full.T


def _mlstm_gates(graw, bias, nb):
    n = graw.shape[1]
    return pl.pallas_call(
        functools.partial(_mlstm_gates_kernel, nb=nb),
        grid=(n // BLK,),
        in_specs=[pl.BlockSpec((N_ML_GATES, BLK), lambda i: (0, i)),
                  pl.BlockSpec((N_ML_GATES, 1), lambda i: (0, 0))],
        out_specs=[pl.BlockSpec((N_ML_GATES, BLK), lambda i: (0, i)),
                   pl.BlockSpec((BLK, LANES), lambda i: (i, 0))],
        out_shape=[jax.ShapeDtypeStruct((N_ML_GATES, n), F32),
                   jax.ShapeDtypeStruct((n, LANES), F32)],
        compiler_params=_params(("parallel",)),
    )(graw, bias)


def _na_bias_tables(rel_bias, rows):
    a = np.arange(NA_GROUP)[:, None]
    u = np.arange(NA_UNION)[None, :]
    half = WIN_H // 2
    vis_first = (u < WIN_H) & (a >= 0)
    dr_first = u - a + WIN_H - 1
    vis_mid = (u >= a) & (u < a + WIN_H)
    dr_mid = u - a + WIN_H - 1 - half
    lo = NA_UNION - WIN_H
    vis_last = (u >= lo) & (a >= 0)
    dr_last = u - a + (WIN_H - 1) - (NA_UNION - NA_GROUP)
    vis_r = np.stack([vis_first, vis_mid, vis_last])
    dr = np.clip(np.stack([dr_first, dr_mid, dr_last]), 0, 2 * WIN_H - 2)
    qc = np.arange(GRID_W)[:, None]
    kc = np.arange(GRID_W)[None, :]
    c0 = np.clip(qc - WIN_W // 2, 0, GRID_W - WIN_W)
    vis_c = (kc >= c0) & (kc < c0 + WIN_W)
    dc = np.clip(kc - qc + WIN_W - 1, 0, 2 * WIN_W - 2)
    shape = (3, NA_GROUP, GRID_W, NA_UNION, GRID_W)
    dr_full = np.broadcast_to(dr[:, :, None, :, None], shape).reshape(3, NA_GROUP * GRID_W, NA_UNION * GRID_W)
    dc_full = np.broadcast_to(dc[None, None, :, None, :], shape).reshape(3, NA_GROUP * GRID_W, NA_UNION * GRID_W)
    vis = (vis_r[:, :, None, :, None] & vis_c[None, None, :, None, :]).reshape(
        3, NA_GROUP * GRID_W, NA_UNION * GRID_W)
    vals = rel_bias.astype(F32)[:, dr_full, dc_full]
    vals = jnp.where(vis[None], vals, -jnp.inf)
    return jnp.transpose(vals, (1, 0, 2, 3))


def _na_kernel(q_ref, k_ref, v_ref, bias_ref, out_ref, *, rows):
    n_groups = rows // NA_GROUP
    gq = NA_GROUP * GRID_W
    uk = NA_UNION * GRID_W
    scale = NA_DIM ** -0.5
    km = k_ref[PAD:BLK, :]
    vm = v_ref[PAD:BLK, :]

    def attend(q, kw, vw, bias):
        s_m = lax.dot_general(q, km, _NT, preferred_element_type=F32) * scale
        m = jnp.max(s_m, axis=-1, keepdims=True)
        if kw is not None:
            s = lax.dot_general(q, kw, _NT, preferred_element_type=F32) * scale + bias
            m = jnp.maximum(m, jnp.max(s, axis=-1, keepdims=True))
        p_m = jnp.exp(s_m - m)
        den = jnp.sum(p_m, axis=-1, keepdims=True)
        y = jnp.dot(p_m.astype(BF16), vm, preferred_element_type=F32)
        if kw is not None:
            p = jnp.exp(s - m)
            den = den + jnp.sum(p, axis=-1, keepdims=True)
            y = y + jnp.dot(p.astype(BF16), vw, preferred_element_type=F32)
        return y / den

    out_ref[0:PAD, :] = jnp.zeros((PAD, NA_DIM), out_ref.dtype)
    out_ref[PAD:BLK, :] = attend(q_ref[PAD:BLK, :], None, None, None).astype(out_ref.dtype)

    def body(g, carry):
        q0 = pl.multiple_of(BLK + g * gq, gq)
        u0 = jnp.clip(g * NA_GROUP - WIN_H // 2, 0, rows - NA_UNION)
        k0 = pl.multiple_of(BLK + u0 * GRID_W, GRID_W)
        cls = jnp.where(g == 0, 0, jnp.where(g == n_groups - 1, 2, 1))
        y = attend(q_ref[pl.ds(q0, gq), :], k_ref[pl.ds(k0, uk), :], v_ref[pl.ds(k0, uk), :],
                   bias_ref[cls])
        out_ref[pl.ds(q0, gq), :] = y.astype(out_ref.dtype)
        return carry

    lax.fori_loop(0, n_groups, body, 0)


def _na_attention(proj, bias, batch, tp):
    rows = (tp - BLK) // GRID_W
    n = proj.shape[0]
    gq = NA_GROUP * GRID_W
    uk = NA_UNION * GRID_W
    return pl.pallas_call(
        functools.partial(_na_kernel, rows=rows),
        grid=(batch, NA_HEADS),
        in_specs=[
            pl.BlockSpec((tp, NA_DIM), lambda b, h: (b, h)),
            pl.BlockSpec((tp, NA_DIM), lambda b, h: (b, NA_HEADS + h)),
            pl.BlockSpec((tp, NA_DIM), lambda b, h: (b, 2 * NA_HEADS + h)),
            pl.BlockSpec((3, None, gq, uk), lambda b, h: (0, h, 0, 0)),
        ],
        out_specs=pl.BlockSpec((tp, NA_DIM), lambda b, h: (b, h)),
        out_shape=jax.ShapeDtypeStruct((n, NA_WIDTH), BF16),
        compiler_params=_params(("parallel", "parallel")),
    )(proj, proj, proj, bias)


def _mlstm_dir(q_ref, k_ref, v_ref, gr_ref, gc_ref, y_ref, c_ref, m_ref, head, rev):
    q = q_ref[...]
    k = k_ref[...]
    v = v_ref[...]
    base = 8 if rev else 0
    ig_r = gr_ref[pl.ds(base + head, 1), :]
    bc_r = gr_ref[pl.ds(base + 4 + head, 1), :]
    gc = gc_ref[...]
    lane = lax.broadcasted_iota(jnp.int32, gc.shape, 1)
    ig_c = jnp.sum(jnp.where(lane == base + head, gc, 0.0), axis=1, keepdims=True)
    bc_c = jnp.sum(jnp.where(lane == base + 4 + head, gc, 0.0), axis=1, keepdims=True)
    m = m_ref[...]
    ti = lax.broadcasted_iota(jnp.int32, (BLK, BLK), 0)
    si = lax.broadcasted_iota(jnp.int32, (BLK, BLK), 1)
    mask = (si >= ti) if rev else (si <= ti)
    d_log = jnp.where(mask, bc_c - bc_r + ig_r, -jnp.inf)
    m_prev = bc_c + m
    m_t = jnp.maximum(m_prev, jnp.max(d_log, axis=-1, keepdims=True))
    w_prev = jnp.exp(m_prev - m_t)
    p = jnp.exp(d_log - m_t) * lax.dot_general(q, k, _NT, preferred_element_type=F32)
    one_col = jnp.where(lax.broadcasted_iota(jnp.int32, (BLK, LANES), 1) == 0, 1.0, 0.0)
    v_ext = jnp.concatenate([v, one_col.astype(BF16)], axis=1)
    c_ext = c_ref[...]
    num = (w_prev * jnp.dot(q, c_ext.astype(BF16), preferred_element_type=F32)
           + jnp.dot(p.astype(BF16), v_ext, preferred_element_type=F32))
    den = num[:, ML_V:ML_V + 1]
    y_ref[...] = num[:, :ML_V] / jnp.maximum(jnp.abs(den), jnp.exp(-m_t))
    b_end = bc_r[:, 0:1] if rev else bc_r[:, BLK - 1:BLK]
    g_c = b_end - bc_c + ig_c
    g_r = b_end - bc_r + ig_r
    m_new = jnp.maximum(b_end + m, jnp.max(g_r, axis=-1, keepdims=True))
    w_state = jnp.exp(b_end + m - m_new)
    w_tok = jnp.exp(g_c - m_new)
    c_ref[...] = w_state * c_ext + lax.dot_general(k, (w_tok * v_ext.astype(F32)).astype(BF16), _TN,
                                                   preferred_element_type=F32)
    m_ref[...] = m_new


def _mlstm_kernel(qf, kf, vf, grf, gcf, qb, kb, vb, grb, gcb, yf, yb, cf, mf, cb, mb):
    head = pl.program_id(1)

    @pl.when(pl.program_id(2) == 0)
    def _():
        cf[...] = jnp.zeros(cf.shape, F32)
        cb[...] = jnp.zeros(cb.shape, F32)
        mf[...] = jnp.zeros(mf.shape, F32)
        mb[...] = jnp.zeros(mb.shape, F32)

    _mlstm_dir(qf, kf, vf, grf, gcf, yf, cf, mf, head, False)
    _mlstm_dir(qb, kb, vb, grb, gcb, yb, cb, mb, head, True)


def _mlstm(proj, gr, gc, batch, nb):
    n = proj.shape[0]
    q0 = 3 * NA_WIDTH // ML_QK
    k0 = q0 + ML_HEADS
    v0 = (3 * NA_WIDTH + 2 * ML_QK_WIDTH) // ML_V
    fwd = lambda b, h, s: b * nb + s
    bwd = lambda b, h, s: b * nb + nb - 1 - s

    def specs(pos):
        return [
            pl.BlockSpec((BLK, ML_QK), lambda b, h, s: (pos(b, h, s), q0 + h)),
            pl.BlockSpec((BLK, ML_QK), lambda b, h, s: (pos(b, h, s), k0 + h)),
            pl.BlockSpec((BLK, ML_V), lambda b, h, s: (pos(b, h, s), v0 + h)),
            pl.BlockSpec((N_ML_GATES, BLK), lambda b, h, s: (0, pos(b, h, s))),
            pl.BlockSpec((BLK, LANES), lambda b, h, s: (pos(b, h, s), 0)),
        ]

    return pl.pallas_call(
        _mlstm_kernel,
        grid=(batch, ML_HEADS, nb),
        in_specs=specs(fwd) + specs(bwd),
        out_specs=[pl.BlockSpec((BLK, ML_V), lambda b, h, s: (fwd(b, h, s), h)),
                   pl.BlockSpec((BLK, ML_V), lambda b, h, s: (bwd(b, h, s), h))],
        out_shape=[jax.ShapeDtypeStruct((n, ML_WIDTH), F32)] * 2,
        scratch_shapes=[pltpu.VMEM((ML_QK, ML_V + LANES), F32), pltpu.VMEM((1, 1), F32),
                        pltpu.VMEM((ML_QK, ML_V + LANES), F32), pltpu.VMEM((1, 1), F32)],
        compiler_params=_params(("parallel", "parallel", "arbitrary")),
    )(proj, proj, proj, gr, gc, proj, proj, proj, gr, gc)


def _gla_dir(q_ref, k_ref, v_ref, lr_ref, gu_ref, gb_ref, bd_ref, o_ref, st_ref, valid, rev):
    z = jnp.dot(lr_ref[...], gu_ref[...], preferred_element_type=F32) + gb_ref[...]
    la = jnp.where(valid, _log_sigmoid(z) * (1.0 / GLA_TAU), 0.0)
    hi = la.astype(BF16)
    lo = (la - hi.astype(F32)).astype(BF16)
    bd = bd_ref[...]
    bcum = (jnp.dot(bd, hi, preferred_element_type=F32) + jnp.dot(bd, lo, preferred_element_type=F32))
    ti = lax.broadcasted_iota(jnp.int32, (GLA_CHUNK, GLA_CHUNK), 0)
    si = lax.broadcasted_iota(jnp.int32, (GLA_CHUNK, GLA_CHUNK), 1)
    mask = (si >= ti) if rev else (si <= ti)
    n_chunks = BLK // GLA_CHUNK
    order = range(n_chunks - 1, -1, -1) if rev else range(n_chunks)
    for c in order:
        sl = slice(c * GLA_CHUNK, (c + 1) * GLA_CHUNK)
        b = bcum[sl]
        q = q_ref[sl, :].astype(F32)
        k = k_ref[sl, :].astype(F32)
        v = v_ref[sl, :]
        b_end = b[0:1] if rev else b[GLA_CHUNK - 1:GLA_CHUNK]
        b_mid = b[GLA_CHUNK // 2 - 1:GLA_CHUNK // 2]
        q_in = (q * jnp.exp(b)).astype(BF16)
        q_mid = (q * jnp.exp(b - b_mid)).astype(BF16)
        k_mid = (k * jnp.exp(b_mid - b)).astype(BF16)
        a = lax.dot_general(q_mid, k_mid, _NT, preferred_element_type=F32)
        a = jnp.where(mask, a, 0.0)
        st = st_ref[...]
        o_ref[sl, :] = (lax.dot_general(q_in, st.astype(BF16), _NT, preferred_element_type=F32)
                        + jnp.dot(a.astype(BF16), v, preferred_element_type=F32))
        k_dec = (k * jnp.exp(b_end - b)).astype(BF16)
        st_ref[...] = st * jnp.exp(b_end) + lax.dot_general(v, k_dec, _TN, preferred_element_type=F32)


def _gla_kernel(qf, kf, vf, lrf, guf, gbf, bdf, qb, kb, vb, lrb, gub, gbb, bdb, of, ob, sf, sb, *, nb):
    s = pl.program_id(2)

    @pl.when(s == 0)
    def _():
        sf[...] = jnp.zeros(sf.shape, F32)
        sb[...] = jnp.zeros(sb.shape, F32)

    row = lax.broadcasted_iota(jnp.int32, (BLK, 1), 0)
    _gla_dir(qf, kf, vf, lrf, guf, gbf, bdf, of, sf, (row + s * BLK) >= PAD, False)
    _gla_dir(qb, kb, vb, lrb, gub, gbb, bdb, ob, sb, (row + (nb - 1 - s) * BLK) >= PAD, True)


def _gla(proj, lr, gate_up, gate_bias, batch, nb):
    n = proj.shape[0]
    k0 = GLA_KW // GLA_DK
    v0 = 2 * GLA_KW // GLA_DV
    fwd = lambda b, h, s: b * nb + s
    bwd = lambda b, h, s: b * nb + nb - 1 - s
    t = np.arange(BLK)
    same = (t[:, None] // GLA_CHUNK) == (t[None, :] // GLA_CHUNK)
    bd_f = jnp.asarray(same & (t[None, :] <= t[:, None]), BF16)
    bd_b = jnp.asarray(same & (t[None, :] >= t[:, None]), BF16)

    def specs(pos, direction):
        return [
            pl.BlockSpec((BLK, GLA_DK), lambda b, h, s: (pos(b, h, s), h)),
            pl.BlockSpec((BLK, GLA_DK), lambda b, h, s: (pos(b, h, s), k0 + h)),
            pl.BlockSpec((BLK, GLA_DV), lambda b, h, s: (pos(b, h, s), v0 + h)),
            pl.BlockSpec((BLK, LANES), lambda b, h, s: (pos(b, h, s), 0)),
            pl.BlockSpec((None, None, LANES, GLA_DK), lambda b, h, s: (direction, h, 0, 0)),
            pl.BlockSpec((None, None, 1, GLA_DK), lambda b, h, s: (direction, h, 0, 0)),
            pl.BlockSpec((BLK, BLK), lambda b, h, s: (0, 0)),
        ]

    return pl.pallas_call(
        functools.partial(_gla_kernel, nb=nb),
        grid=(batch, GLA_HEADS, nb),
        in_specs=specs(fwd, 0) + specs(bwd, 1),
        out_specs=[pl.BlockSpec((BLK, GLA_DV), lambda b, h, s: (fwd(b, h, s), h)),
                   pl.BlockSpec((BLK, GLA_DV), lambda b, h, s: (bwd(b, h, s), h))],
        out_shape=[jax.ShapeDtypeStruct((n, GLA_VW), F32)] * 2,
        scratch_shapes=[pltpu.VMEM((GLA_DV, GLA_DK), F32), pltpu.VMEM((GLA_DV, GLA_DK), F32)],
        compiler_params=_params(("parallel", "parallel", "arbitrary")),
    )(proj, proj, proj, lr, gate_up, gate_bias, bd_f, proj, proj, proj, lr, gate_up, gate_bias, bd_b)


def _gated_head_rms(yf_ref, yb_ref, gate_ref, gain_ref, n_heads, act):
    y = yf_ref[...] + yb_ref[...]
    gate = act(gate_ref[...].astype(F32))
    gain = gain_ref[...]
    hd = y.shape[1] // n_heads
    outs = []
    for hh in range(n_heads):
        sl = slice(hh * hd, (hh + 1) * hd)
        a = y[:, sl]
        ms = jnp.mean(a * a, axis=-1, keepdims=True)
        outs.append((a * lax.rsqrt(ms + EPS) * gain[:, sl] * gate[:, sl]).astype(BF16))
    return jnp.concatenate(outs, axis=1)


def _out_proj0_kernel(ya_ref, yf_ref, yb_ref, gate_ref, gain_ref, wa_ref, wb_ref, h_ref, out_ref, *,
                      tiles_per_batch):
    tm = h_ref.shape[0]
    y_ml = _gated_head_rms(yf_ref, yb_ref, gate_ref, gain_ref, ML_HEADS, _sigmoid)
    out = (h_ref[...] + jnp.dot(ya_ref[...], wa_ref[...], preferred_element_type=F32)
           + jnp.dot(y_ml, wb_ref[...], preferred_element_type=F32))
    out_ref[...] = jnp.where(_row_valid(pl.program_id(0), tiles_per_batch, tm), out, 0.0)


def _out_proj1_kernel(yf_ref, yb_ref, gate_ref, gain_ref, w_ref, h_ref, out_ref, *, tiles_per_batch):
    tm = h_ref.shape[0]
    y = _gated_head_rms(yf_ref, yb_ref, gate_ref, gain_ref, GLA_HEADS, lambda r: r * _sigmoid(r))
    out = h_ref[...] + jnp.dot(y, w_ref[...], preferred_element_type=F32)
    out_ref[...] = jnp.where(_row_valid(pl.program_id(0), tiles_per_batch, tm), out, 0.0)


def _out_proj0(y_na, y_f, y_b, proj, gain, w_a, w_b, h, tp, tm):
    n, d = h.shape
    gate_blk = (3 * NA_WIDTH + 2 * ML_QK_WIDTH + ML_WIDTH) // ML_WIDTH
    row = lambda i: (i, 0)
    const = lambda i: (0, 0)
    return pl.pallas_call(
        functools.partial(_out_proj0_kernel, tiles_per_batch=tp // tm),
        grid=(n // tm,),
        in_specs=[
            pl.BlockSpec((tm, NA_WIDTH), row),
            pl.BlockSpec((tm, ML_WIDTH), row),
            pl.BlockSpec((tm, ML_WIDTH), row),
            pl.BlockSpec((tm, ML_WIDTH), lambda i: (i, gate_blk)),
            pl.BlockSpec((1, ML_WIDTH), const),
            pl.BlockSpec((NA_WIDTH, d), const),
            pl.BlockSpec((ML_WIDTH, d), const),
            pl.BlockSpec((tm, d), row),
        ],
        out_specs=pl.BlockSpec((tm, d), row),
        out_shape=jax.ShapeDtypeStruct((n, d), F32),
        compiler_params=_params(("parallel",)),
    )(y_na, y_f, y_b, proj, gain, w_a, w_b, h)


def _out_proj1(o_f, o_b, proj, gain, w, h, tp, tm):
    n, d = h.shape
    gate_blk = (2 * GLA_KW + GLA_VW) // GLA_VW
    row = lambda i: (i, 0)
    const = lambda i: (0, 0)
    return pl.pallas_call(
        functools.partial(_out_proj1_kernel, tiles_per_batch=tp // tm),
        grid=(n // tm,),
        in_specs=[
            pl.BlockSpec((tm, GLA_VW), row),
            pl.BlockSpec((tm, GLA_VW), row),
            pl.BlockSpec((tm, GLA_VW), lambda i: (i, gate_blk)),
            pl.BlockSpec((1, GLA_VW), const),
            pl.BlockSpec((GLA_VW, d), const),
            pl.BlockSpec((tm, d), row),
        ],
        out_specs=pl.BlockSpec((tm, d), row),
        out_shape=jax.ShapeDtypeStruct((n, d), F32),
        compiler_params=_params(("parallel",)),
    )(o_f, o_b, proj, gain, w, h)


def _ffn_kernel(h_ref, hp_ref, hn_ref, gain_ref, wg_ref, wv_ref, cwg_ref, cwv_ref, cbg_ref, cbv_ref,
                wd_ref, out_ref, xn_ref, acc_ref, *, tiles_per_batch, rc):
    i = pl.program_id(0)
    c = pl.program_id(1)
    tm = h_ref.shape[0]

    @pl.when(c == 0)
    def _():
        _rms_rows(h_ref, gain_ref, xn_ref, HALO, tm, rc)
        _rms_rows(hp_ref, gain_ref, xn_ref, 0, HALO, HALO)
        _rms_rows(hn_ref, gain_ref, xn_ref, HALO + tm, HALO, HALO)

        @pl.when(i == 0)
        def _():
            xn_ref[0:HALO, :] = jnp.zeros((HALO, xn_ref.shape[1]), xn_ref.dtype)

        @pl.when(i == pl.num_programs(0) - 1)
        def _():
            xn_ref[HALO + tm:, :] = jnp.zeros((HALO, xn_ref.shape[1]), xn_ref.dtype)

        acc_ref[...] = jnp.zeros(acc_ref.shape, F32)

    xn = xn_ref[...]

    def conv(w_ref, cw_ref, cb_ref):
        u = jnp.dot(xn, w_ref[...], preferred_element_type=F32)
        cw = cw_ref[...]
        out = cb_ref[...]
        for tap in range(CONV_W):
            off = HALO - CONV_W // 2 + tap
            out = out + u[off:off + tm] * cw[tap:tap + 1]
        return out

    g = conv(wg_ref, cwg_ref, cbg_ref)
    val = conv(wv_ref, cwv_ref, cbv_ref)
    act = (g * _sigmoid(g) * val).astype(BF16)
    acc_ref[...] += jnp.dot(act, wd_ref[...], preferred_element_type=F32)

    @pl.when(c == pl.num_programs(1) - 1)
    def _():
        out_ref[...] = jnp.where(_row_valid(i, tiles_per_batch, tm), h_ref[...] + acc_ref[...], 0.0)


def _ffn(h, gain, w_up, conv_w, conv_b, w_down, tp, tm, tf):
    n, d = h.shape
    d_ff = w_down.shape[0]
    n_ff = d_ff // tf
    hb = tm // HALO
    last_halo = n // HALO - 1
    return pl.pallas_call(
        functools.partial(_ffn_kernel, tiles_per_batch=tp // tm, rc=128),
        grid=(n // tm, n_ff),
        in_specs=[
            pl.BlockSpec((tm, d), lambda i, c: (i, 0)),
            pl.BlockSpec((HALO, d), lambda i, c: (jnp.maximum(i * hb - 1, 0), 0)),
            pl.BlockSpec((HALO, d), lambda i, c: (jnp.minimum((i + 1) * hb, last_halo), 0)),
            pl.BlockSpec((1, d), lambda i, c: (0, 0)),
            pl.BlockSpec((d, tf), lambda i, c: (0, c)),
            pl.BlockSpec((d, tf), lambda i, c: (0, n_ff + c)),
            pl.BlockSpec((CONV_W, tf), lambda i, c: (0, c)),
            pl.BlockSpec((CONV_W, tf), lambda i, c: (0, n_ff + c)),
            pl.BlockSpec((1, tf), lambda i, c: (0, c)),
            pl.BlockSpec((1, tf), lambda i, c: (0, n_ff + c)),
            pl.BlockSpec((tf, d), lambda i, c: (c, 0)),
        ],
        out_specs=pl.BlockSpec((tm, d), lambda i, c: (i, 0)),
        out_shape=jax.ShapeDtypeStruct((n, d), F32),
        scratch_shapes=[pltpu.VMEM((tm + 2 * HALO, d), BF16), pltpu.VMEM((tm, d), F32)],
        compiler_params=_params(("parallel", "arbitrary")),
    )(h, h, h, gain, w_up, w_up, conv_w, conv_w, conv_b, conv_b, w_down)


def kernel(x, meta_tokens, norm_mix, norm_ffn, ab_w_in, ab_gate_bias, ab_q_gain, ab_k_gain, ab_rel_bias, ab_ml_gain, ab_w_out, c_w_in, c_gate_up, c_gate_bias, c_head_gain, c_w_out, ffn_w_up, ffn_conv_w, ffn_conv_b, ffn_w_down):
    batch, seq, d = x.shape
    d_ff = ffn_w_down.shape[1]
    assert seq % (NA_GROUP * GRID_W) == 0 and seq // GRID_W >= NA_UNION and d % LANES == 0
    tp = BLK + seq
    nb = tp // BLK
    n = batch * tp
    tm_proj = _pick_tile(tp, (1280, 1024, 768, 512, 256))
    tm_out = _pick_tile(tp, (512, 256))
    tm_ffn = _pick_tile(tp, (640, 512, 256))
    tn_proj = 1024
    tf = _pick_tile(d_ff, (512, 256, 128))

    meta = jnp.broadcast_to(meta_tokens.astype(x.dtype)[None], (batch, N_META, d))
    h = jnp.concatenate([jnp.zeros((batch, PAD, d), x.dtype), meta, x], axis=1).reshape(n, d)

    n_main0 = 3 * NA_WIDTH + 2 * ML_QK_WIDTH + 2 * ML_WIDTH
    w_in = ab_w_in[0]
    ones = lambda k: jnp.ones((k,), F32)
    colscale0 = jnp.concatenate([
        jnp.tile(ab_q_gain[0].astype(F32), NA_HEADS), jnp.tile(ab_k_gain[0].astype(F32), NA_HEADS),
        ones(NA_WIDTH), ones(ML_QK_WIDTH), jnp.full((ML_QK_WIDTH,), ML_QK ** -0.5, F32),
        ones(2 * ML_WIDTH)])[None]
    proj0, graw = _in_proj(h, norm_mix[0][None].astype(F32), w_in[:, :n_main0].astype(BF16), colscale0,
                           w_in[:, n_main0:].T.astype(BF16), n_rms=2 * NA_WIDTH // tn_proj, tail_nt=True,
                           tm=tm_proj, tn=tn_proj)
    gr, gc = _mlstm_gates(graw, ab_gate_bias[0].astype(F32)[:, None], nb)
    y_na = _na_attention(proj0, _na_bias_tables(ab_rel_bias[0], seq // GRID_W), batch, tp)
    y_f, y_b = _mlstm(proj0, gr, gc, batch, nb)
    w_out = ab_w_out[0].astype(BF16)
    h = _out_proj0(y_na, y_f, y_b, proj0, ab_ml_gain[0][None].astype(F32), w_out[:NA_WIDTH], w_out[NA_WIDTH:],
                   h, tp, tm_out)
    h = _ffn(h, norm_ffn[0][None].astype(F32), ffn_w_up[0].astype(BF16), ffn_conv_w[0].astype(F32),
             ffn_conv_b[0][None].astype(F32), ffn_w_down[0].astype(BF16), tp, tm_ffn, tf)

    n_main1 = 2 * GLA_KW + 2 * GLA_VW
    w_in = c_w_in[0]
    colscale1 = jnp.concatenate([jnp.full((GLA_KW,), GLA_DK ** -0.5, F32), ones(n_main1 - GLA_KW)])[None]
    w_tail = jnp.pad(w_in[:, n_main1:], ((0, 0), (0, LANES - 2 * GLA_RANK))).astype(BF16)
    proj1, lr = _in_proj(h, norm_mix[1][None].astype(F32), w_in[:, :n_main1].astype(BF16), colscale1, w_tail,
                         n_rms=0, tail_nt=False, tm=tm_proj, tn=tn_proj)
    gu = c_gate_up[0].astype(BF16).reshape(2, GLA_RANK, GLA_HEADS, GLA_DK).transpose(0, 2, 1, 3)
    gate_up = jnp.stack([jnp.pad(gu[0], ((0, 0), (0, LANES - GLA_RANK), (0, 0))),
                         jnp.pad(gu[1], ((0, 0), (GLA_RANK, LANES - 2 * GLA_RANK), (0, 0)))])
    gate_bias = c_gate_bias[0].astype(F32).reshape(2, GLA_HEADS, 1, GLA_DK)
    o_f, o_b = _gla(proj1, lr, gate_up, gate_bias, batch, nb)
    h = _out_proj1(o_f, o_b, proj1, c_head_gain[0][None].astype(F32), c_w_out[0].astype(BF16), h, tp, tm_out)
    h = _ffn(h, norm_ffn[1][None].astype(F32), ffn_w_up[1].astype(BF16), ffn_conv_w[1].astype(F32),
             ffn_conv_b[1][None].astype(F32), ffn_w_down[1].astype(BF16), tp, tm_ffn, tf)
    return h.reshape(batch, tp, d)[:, BLK:]
```

```python
import functools

import numpy as np
import jax
import jax.numpy as jnp
from jax import lax
from jax.experimental import pallas as pl
from jax.experimental.pallas import tpu as pltpu

F32 = jnp.float32
BF16 = jnp.bfloat16
EPS = 1e-6

N_META = 16
GRID_W = 64
NA_HEADS = 8
NA_DIM = 128
NA_WIDTH = NA_HEADS * NA_DIM
WIN_H = 8
WIN_W = 16
ML_HEADS = 4
ML_QK = 128
ML_V = 256
ML_QK_WIDTH = ML_HEADS * ML_QK
ML_WIDTH = ML_HEADS * ML_V
N_ML_GATES = 4 * ML_HEADS
GLA_HEADS = 4
GLA_DK = 256
GLA_DV = 512
GLA_KW = GLA_HEADS * GLA_DK
GLA_VW = GLA_HEADS * GLA_DV
GLA_RANK = 16
GLA_TAU = 16.0
GLA_CHUNK = 64
CONV_W = 3

LANES = 128
BLK = 256
PAD = BLK - N_META
NA_GROUP = 4
NA_UNION = NA_GROUP + WIN_H - 1
HALO = 16
VMEM_LIMIT = 56 * 1024 * 1024

_NT = (((1,), (1,)), ((), ()))
_TN = (((0,), (0,)), ((), ()))


def _log_sigmoid(x):
    return jnp.minimum(x, 0.0) - jnp.log(1.0 + jnp.exp(-jnp.abs(x)))


def _sigmoid(x):
    return 1.0 / (1.0 + jnp.exp(-x))


def _pick_tile(n, prefs):
    for t in prefs:
        if n % t == 0:
            return t
    raise ValueError(f"no tile in {prefs} divides {n}")


def _params(sem, vmem=VMEM_LIMIT):
    return pltpu.CompilerParams(dimension_semantics=sem, vmem_limit_bytes=vmem)


def _rms_rows(x_ref, gain_ref, dst_ref, dst_row0, n_rows, rc):
    g = gain_ref[...]

    def body(r, carry):
        r0 = pl.multiple_of(r * rc, rc)
        xx = x_ref[pl.ds(r0, rc), :]
        ms = jnp.mean(xx * xx, axis=-1, keepdims=True)
        dst_ref[pl.ds(dst_row0 + r0, rc), :] = (xx * lax.rsqrt(ms + EPS) * g).astype(dst_ref.dtype)
        return carry

    lax.fori_loop(0, n_rows // rc, body, 0)


def _row_valid(tile_idx, tiles_per_batch, tm):
    it = tile_idx % tiles_per_batch
    row = lax.broadcasted_iota(jnp.int32, (tm, 1), 0)
    return (row + it * tm) >= PAD


def _in_proj_kernel(x_ref, gain_ref, w_ref, cs_ref, wt_ref, out_ref, tail_ref, xn_ref, *,
                    n_rms, tail_nt, rc):
    j = pl.program_id(1)
    tm = x_ref.shape[0]
    tn = w_ref.shape[1]

    @pl.when(j == 0)
    def _():
        _rms_rows(x_ref, gain_ref, xn_ref, 0, tm, rc)
        if tail_nt:
            tail_ref[...] = lax.dot_general(wt_ref[...], xn_ref[...], _NT, preferred_element_type=F32)
        else:
            tail_ref[...] = jnp.dot(xn_ref[...], wt_ref[...],
                                    preferred_element_type=F32).astype(tail_ref.dtype)

    @pl.when(j > 0)
    def _():
        acc = jnp.dot(xn_ref[...], w_ref[...], preferred_element_type=F32)
        cs = cs_ref[...]
        if n_rms > 0:
            @pl.when(j <= n_rms)
            def _():
                for hh in range(tn // LANES):
                    sl = slice(hh * LANES, (hh + 1) * LANES)
                    a = acc[:, sl]
                    ms = jnp.mean(a * a, axis=-1, keepdims=True)
                    out_ref[:, sl] = (a * lax.rsqrt(ms + EPS) * cs[:, sl]).astype(out_ref.dtype)

            @pl.when(j > n_rms)
            def _():
                out_ref[...] = (acc * cs).astype(out_ref.dtype)
        else:
            out_ref[...] = (acc * cs).astype(out_ref.dtype)


def _in_proj(h, gain, w_main, colscale, w_tail, *, n_rms, tail_nt, tm, tn):
    n, d = h.shape
    n_main = w_main.shape[1] // tn
    if tail_nt:
        n_tail = w_tail.shape[0]
        wt_spec = pl.BlockSpec((n_tail, d), lambda i, j: (0, 0))
        tail_shape = jax.ShapeDtypeStruct((n_tail, n), F32)
        tail_spec = pl.BlockSpec((n_tail, tm), lambda i, j: (0, i))
    else:
        n_tail = w_tail.shape[1]
        wt_spec = pl.BlockSpec((d, n_tail), lambda i, j: (0, 0))
        tail_shape = jax.ShapeDtypeStruct((n, n_tail), BF16)
        tail_spec = pl.BlockSpec((tm, n_tail), lambda i, j: (i, 0))
    col = lambda i, j: (0, jnp.maximum(j - 1, 0))
    kern = functools.partial(_in_proj_kernel, n_rms=n_rms, tail_nt=tail_nt, rc=128)
    return pl.pallas_call(
        kern,
        grid=(n // tm, n_main + 1),
        in_specs=[
            pl.BlockSpec((tm, d), lambda i, j: (i, 0)),
            pl.BlockSpec((1, d), lambda i, j: (0, 0)),
            pl.BlockSpec((d, tn), col),
            pl.BlockSpec((1, tn), col),
            wt_spec,
        ],
        out_specs=[
            pl.BlockSpec((tm, tn), lambda i, j: (i, jnp.maximum(j - 1, 0))),
            tail_spec,
        ],
        out_shape=[jax.ShapeDtypeStruct((n, w_main.shape[1]), BF16), tail_shape],
        scratch_shapes=[pltpu.VMEM((tm, d), BF16)],
        compiler_params=_params(("parallel", "arbitrary")),
    )(h, gain, w_main, colscale, w_tail)


def _mlstm_gates_kernel(graw_ref, bias_ref, gr_ref, gc_ref, *, nb):
    blk = pl.program_id(0) % nb
    g = graw_ref[...] + bias_ref[...]
    row = lax.broadcasted_iota(jnp.int32, g.shape, 0)
    lane = lax.broadcasted_iota(jnp.int32, g.shape, 1)
    is_forget = (row & 4) != 0
    valid = (lane + blk * BLK) >= PAD
    lf = jnp.where(valid, _log_sigmoid(g), 0.0)
    pre = lf
    suf = lf
    d = 1
    while d < BLK:
        pre = pre + jnp.where(lane >= d, pltpu.roll(pre, d, axis=1), 0.0)
        suf = suf + jnp.where(lane < BLK - d, pltpu.roll(suf, BLK - d, axis=1), 0.0)
        d *= 2
    ig = jnp.where(valid, g, -jnp.inf)
    out = jnp.where(is_forget, jnp.where(row >= 8, suf, pre), ig)
    gr_ref[...] = out
    full = jnp.concatenate([out, jnp.zeros((LANES - N_ML_GATES, BLK), F32)], axis=0)
    gc_ref[...] = full.T


def _mlstm_gates(graw, bias, nb):
    n = graw.shape[1]
    return pl.pallas_call(
        functools.partial(_mlstm_gates_kernel, nb=nb),
        grid=(n // BLK,),
        in_specs=[pl.BlockSpec((N_ML_GATES, BLK), lambda i: (0, i)),
                  pl.BlockSpec((N_ML_GATES, 1), lambda i: (0, 0))],
        out_specs=[pl.BlockSpec((N_ML_GATES, BLK), lambda i: (0, i)),
                   pl.BlockSpec((BLK, LANES), lambda i: (i, 0))],
        out_shape=[jax.ShapeDtypeStruct((N_ML_GATES, n), F32),
                   jax.ShapeDtypeStruct((n, LANES), F32)],
        compiler_params=_params(("parallel",)),
    )(graw, bias)


def _na_bias_tables(rel_bias, rows):
    a = np.arange(NA_GROUP)[:, None]
    u = np.arange(NA_UNION)[None, :]
    half = WIN_H // 2
    vis_first = (u < WIN_H) & (a >= 0)
    dr_first = u - a + WIN_H - 1
    vis_mid = (u >= a) & (u < a + WIN_H)
    dr_mid = u - a + WIN_H - 1 - half
    lo = NA_UNION - WIN_H
    vis_last = (u >= lo) & (a >= 0)
    dr_last = u - a + (WIN_H - 1) - (NA_UNION - NA_GROUP)
    vis_r = np.stack([vis_first, vis_mid, vis_last])
    dr = np.clip(np.stack([dr_first, dr_mid, dr_last]), 0, 2 * WIN_H - 2)
    qc = np.arange(GRID_W)[:, None]
    kc = np.arange(GRID_W)[None, :]
    c0 = np.clip(qc - WIN_W // 2, 0, GRID_W - WIN_W)
    vis_c = (kc >= c0) & (kc < c0 + WIN_W)
    dc = np.clip(kc - qc + WIN_W - 1, 0, 2 * WIN_W - 2)
    sel_r = (dr[..., None] == np.arange(2 * WIN_H - 1)).astype(np.float32)
    sel_c = (dc[..., None] == np.arange(2 * WIN_W - 1)).astype(np.float32)
    cols = jnp.einsum('hrd,qkd->hrqk', rel_bias.astype(F32), sel_c, precision=lax.Precision.HIGHEST)
    vals = jnp.einsum('caur,hrqk->chaquk', sel_r, cols, precision=lax.Precision.HIGHEST)
    vis = vis_r[:, None, :, None, :, None] & vis_c[None, None, None, :, None, :]
    vals = jnp.where(vis, vals, -jnp.inf)
    return vals.reshape(3, NA_HEADS, NA_GROUP * GRID_W, NA_UNION * GRID_W)


def _na_kernel(q_ref, k_ref, v_ref, bias_ref, out_ref, *, rows):
    n_groups = rows // NA_GROUP
    gq = NA_GROUP * GRID_W
    uk = NA_UNION * GRID_W
    scale = NA_DIM ** -0.5
    km = k_ref[PAD:BLK, :]
    vm = v_ref[PAD:BLK, :]

    def attend(q, kw, vw, bias):
        s_m = lax.dot_general(q, km, _NT, preferred_element_type=F32) * scale
        m = jnp.max(s_m, axis=-1, keepdims=True)
        if kw is not None:
            s = lax.dot_general(q, kw, _NT, preferred_element_type=F32) * scale + bias
            m = jnp.maximum(m, jnp.max(s, axis=-1, keepdims=True))
        p_m = jnp.exp(s_m - m)
        den = jnp.sum(p_m, axis=-1, keepdims=True)
        y = jnp.dot(p_m.astype(BF16), vm, preferred_element_type=F32)
        if kw is not None:
            p = jnp.exp(s - m)
            den = den + jnp.sum(p, axis=-1, keepdims=True)
            y = y + jnp.dot(p.astype(BF16), vw, preferred_element_type=F32)
        return y / den

    out_ref[0:PAD, :] = jnp.zeros((PAD, NA_DIM), out_ref.dtype)
    out_ref[PAD:BLK, :] = attend(q_ref[PAD:BLK, :], None, None, None).astype(out_ref.dtype)

    def body(g, carry):
        q0 = pl.multiple_of(BLK + g * gq, gq)
        u0 = jnp.clip(g * NA_GROUP - WIN_H // 2, 0, rows - NA_UNION)
        k0 = pl.multiple_of(BLK + u0 * GRID_W, GRID_W)
        cls = jnp.where(g == 0, 0, jnp.where(g == n_groups - 1, 2, 1))
        y = attend(q_ref[pl.ds(q0, gq), :], k_ref[pl.ds(k0, uk), :], v_ref[pl.ds(k0, uk), :],
                   bias_ref[cls])
        out_ref[pl.ds(q0, gq), :] = y.astype(out_ref.dtype)
        return carry

    lax.fori_loop(0, n_groups, body, 0)


def _na_attention(proj, bias, batch, tp):
    rows = (tp - BLK) // GRID_W
    n = proj.shape[0]
    gq = NA_GROUP * GRID_W
    uk = NA_UNION * GRID_W
    return pl.pallas_call(
        functools.partial(_na_kernel, rows=rows),
        grid=(batch, NA_HEADS),
        in_specs=[
            pl.BlockSpec((tp, NA_DIM), lambda b, h: (b, h)),
            pl.BlockSpec((tp, NA_DIM), lambda b, h: (b, NA_HEADS + h)),
            pl.BlockSpec((tp, NA_DIM), lambda b, h: (b, 2 * NA_HEADS + h)),
            pl.BlockSpec((3, None, gq, uk), lambda b, h: (0, h, 0, 0)),
        ],
        out_specs=pl.BlockSpec((tp, NA_DIM), lambda b, h: (b, h)),
        out_shape=jax.ShapeDtypeStruct((n, NA_WIDTH), BF16),
        compiler_params=_params(("parallel", "parallel")),
    )(proj, proj, proj, bias)


def _mlstm_dir(q_ref, k_ref, v_ref, gr_ref, gc_ref, y_ref, c_ref, m_ref, head, rev):
    q = q_ref[:, head * ML_QK:(head + 1) * ML_QK]
    k = k_ref[:, head * ML_QK:(head + 1) * ML_QK]
    v = v_ref[:, head * ML_V:(head + 1) * ML_V]
    base = 8 if rev else 0
    ig_r = gr_ref[base + head:base + head + 1, :]
    bc_r = gr_ref[base + 4 + head:base + 5 + head, :]
    ig_c = gc_ref[:, base + head:base + head + 1]
    bc_c = gc_ref[:, base + 4 + head:base + 5 + head]
    m = m_ref[head]
    ti = lax.broadcasted_iota(jnp.int32, (BLK, BLK), 0)
    si = lax.broadcasted_iota(jnp.int32, (BLK, BLK), 1)
    mask = (si >= ti) if rev else (si <= ti)
    d_log = jnp.where(mask, bc_c - bc_r + ig_r, -jnp.inf)
    m_prev = bc_c + m
    m_t = jnp.maximum(m_prev, jnp.max(d_log, axis=-1, keepdims=True))
    w_prev = jnp.exp(m_prev - m_t)
    p = jnp.exp(d_log - m_t) * lax.dot_general(q, k, _NT, preferred_element_type=F32)
    one_col = jnp.where(lax.broadcasted_iota(jnp.int32, (BLK, LANES), 1) == 0, 1.0, 0.0)
    v_ext = jnp.concatenate([v, one_col.astype(BF16)], axis=1)
    c_ext = c_ref[head]
    num = (w_prev * jnp.dot(q, c_ext.astype(BF16), preferred_element_type=F32)
           + jnp.dot(p.astype(BF16), v_ext, preferred_element_type=F32))
    den = num[:, ML_V:ML_V + 1]
    y_ref[:, head * ML_V:(head + 1) * ML_V] = (
        num[:, :ML_V] / jnp.maximum(jnp.abs(den), jnp.exp(-m_t))).astype(y_ref.dtype)
    b_end = bc_r[:, 0:1] if rev else bc_r[:, BLK - 1:BLK]
    g_c = b_end - bc_c + ig_c
    g_r = b_end - bc_r + ig_r
    m_new = jnp.maximum(b_end + m, jnp.max(g_r, axis=-1, keepdims=True))
    w_state = jnp.exp(b_end + m - m_new)
    w_tok = jnp.exp(g_c - m_new)
    c_ref[head] = w_state * c_ext + lax.dot_general(k, (w_tok * v_ext.astype(F32)).astype(BF16), _TN,
                                                    preferred_element_type=F32)
    m_ref[head] = m_new


def _mlstm_kernel(qf, kf, vf, grf, gcf, qb, kb, vb, grb, gcb, yf, yb, cf, mf, cb, mb):
    @pl.when(pl.program_id(1) == 0)
    def _():
        cf[...] = jnp.zeros(cf.shape, F32)
        cb[...] = jnp.zeros(cb.shape, F32)
        mf[...] = jnp.zeros(mf.shape, F32)
        mb[...] = jnp.zeros(mb.shape, F32)

    for head in range(ML_HEADS):
        _mlstm_dir(qf, kf, vf, grf, gcf, yf, cf, mf, head, False)
        _mlstm_dir(qb, kb, vb, grb, gcb, yb, cb, mb, head, True)


def _mlstm(proj, gr, gc, batch, nb):
    n = proj.shape[0]
    q0 = 3 * NA_WIDTH // ML_QK_WIDTH
    k0 = q0 + 1
    v0 = (3 * NA_WIDTH + 2 * ML_QK_WIDTH) // ML_WIDTH
    fwd = lambda b, s: b * nb + s
    bwd = lambda b, s: b * nb + nb - 1 - s

    def specs(pos):
        return [
            pl.BlockSpec((BLK, ML_QK_WIDTH), lambda b, s: (pos(b, s), q0)),
            pl.BlockSpec((BLK, ML_QK_WIDTH), lambda b, s: (pos(b, s), k0)),
            pl.BlockSpec((BLK, ML_WIDTH), lambda b, s: (pos(b, s), v0)),
            pl.BlockSpec((N_ML_GATES, BLK), lambda b, s: (0, pos(b, s))),
            pl.BlockSpec((BLK, LANES), lambda b, s: (pos(b, s), 0)),
        ]

    state = [pltpu.VMEM((ML_HEADS, ML_QK, ML_V + LANES), F32), pltpu.VMEM((ML_HEADS, 1, 1), F32)]
    return pl.pallas_call(
        _mlstm_kernel,
        grid=(batch, nb),
        in_specs=specs(fwd) + specs(bwd),
        out_specs=[pl.BlockSpec((BLK, ML_WIDTH), lambda b, s: (fwd(b, s), 0)),
                   pl.BlockSpec((BLK, ML_WIDTH), lambda b, s: (bwd(b, s), 0))],
        out_shape=[jax.ShapeDtypeStruct((n, ML_WIDTH), BF16)] * 2,
        scratch_shapes=state + state,
        compiler_params=_params(("parallel", "arbitrary")),
    )(proj, proj, proj, gr, gc, proj, proj, proj, gr, gc)


def _gla_dir(q_ref, k_ref, v_ref, lr_ref, gu_ref, gb_ref, tri_ref, o_ref, st_ref, valid, rev):
    dk = q_ref.shape[1]
    z = jnp.dot(lr_ref[...], gu_ref[...], preferred_element_type=F32) + gb_ref[...]
    la = jnp.where(valid, _log_sigmoid(z) * (1.0 / GLA_TAU), 0.0)
    t1 = la.astype(BF16)
    r1 = la - t1.astype(F32)
    t2 = r1.astype(BF16)
    t3 = (r1 - t2.astype(F32)).astype(BF16)
    tri = tri_ref[...]
    b = (jnp.dot(tri, t1, preferred_element_type=F32) + jnp.dot(tri, t2, preferred_element_type=F32)
         + jnp.dot(tri, t3, preferred_element_type=F32))
    q = q_ref[...].astype(F32)
    k = k_ref[...].astype(F32)
    b_end = b[0:1] if rev else b[BLK - 1:BLK]
    q_in = (q * jnp.exp(b)).astype(BF16)
    k_dec = (k * jnp.exp(b_end - b)).astype(BF16)
    sub = GLA_CHUNK
    ti = lax.broadcasted_iota(jnp.int32, (sub, BLK), 0)
    si = lax.broadcasted_iota(jnp.int32, (sub, BLK), 1)

    def at_rows(x, row0):
        parts = [jnp.zeros((n, dk), BF16) for n in (row0,) if n] + [x]
        after = BLK - row0 - x.shape[0]
        if after:
            parts.append(jnp.zeros((after, dk), BF16))
        return jnp.concatenate(parts, axis=0) if len(parts) > 1 else x

    a_rows = []
    for i in range(BLK // sub):
        lo, hi = i * sub, (i + 1) * sub
        b_i, q_i = b[lo:hi], q[lo:hi]
        mu = b[lo + sub // 2:lo + sub // 2 + 1]
        lhs = (q_i * jnp.exp(b_i - mu)).astype(BF16)
        rhs = at_rows((k[lo:hi] * jnp.exp(mu - b_i)).astype(BF16), lo)
        n_prev = BLK - hi if rev else lo
        if n_prev:
            prev = slice(hi, BLK) if rev else slice(0, lo)
            rho = b[hi:hi + 1] if rev else b[lo - 1:lo]
            k_prev = at_rows((k[prev] * jnp.exp(rho - b[prev])).astype(BF16), hi if rev else 0)
            lhs = jnp.concatenate([(q_i * jnp.exp(b_i - rho)).astype(BF16), lhs], axis=1)
            rhs = jnp.concatenate([k_prev, rhs], axis=1)
        a_i = lax.dot_general(lhs, rhs, _NT, preferred_element_type=F32)
        seen = (si >= ti + lo) if rev else (si <= ti + lo)
        a_rows.append(jnp.where(seen, a_i, 0.0).astype(BF16))
    a = jnp.concatenate(a_rows, axis=0)
    st = st_ref[...]
    v = v_ref[...]
    o_ref[...] = (lax.dot_general(q_in, st.astype(BF16), _NT, preferred_element_type=F32)
                  + jnp.dot(a, v, preferred_element_type=F32)).astype(o_ref.dtype)
    st_ref[...] = st * jnp.exp(b_end) + lax.dot_general(v, k_dec, _TN, preferred_element_type=F32)


def _gla_kernel(qf, kf, vf, lrf, guf, gbf, bdf, qb, kb, vb, lrb, gub, gbb, bdb, of, ob, sf, sb, *, nb):
    s = pl.program_id(2)

    @pl.when(s == 0)
    def _():
        sf[...] = jnp.zeros(sf.shape, F32)
        sb[...] = jnp.zeros(sb.shape, F32)

    row = lax.broadcasted_iota(jnp.int32, (BLK, 1), 0)
    _gla_dir(qf, kf, vf, lrf, guf, gbf, bdf, of, sf, (row + s * BLK) >= PAD, False)
    _gla_dir(qb, kb, vb, lrb, gub, gbb, bdb, ob, sb, (row + (nb - 1 - s) * BLK) >= PAD, True)


def _gla(proj, lr, gate_up, gate_bias, batch, nb):
    n = proj.shape[0]
    k0 = GLA_KW // GLA_DK
    v0 = 2 * GLA_KW // GLA_DV
    fwd = lambda b, h, s: b * nb + s
    bwd = lambda b, h, s: b * nb + nb - 1 - s
    t = np.arange(BLK)
    bd_f = jnp.asarray(t[None, :] <= t[:, None], BF16)
    bd_b = jnp.asarray(t[None, :] >= t[:, None], BF16)

    def specs(pos, direction):
        return [
            pl.BlockSpec((BLK, GLA_DK), lambda b, h, s: (pos(b, h, s), h)),
            pl.BlockSpec((BLK, GLA_DK), lambda b, h, s: (pos(b, h, s), k0 + h)),
            pl.BlockSpec((BLK, GLA_DV), lambda b, h, s: (pos(b, h, s), v0 + h)),
            pl.BlockSpec((BLK, LANES), lambda b, h, s: (pos(b, h, s), 0)),
            pl.BlockSpec((None, None, LANES, GLA_DK), lambda b, h, s: (direction, h, 0, 0)),
            pl.BlockSpec((None, None, 1, GLA_DK), lambda b, h, s: (direction, h, 0, 0)),
            pl.BlockSpec((BLK, BLK), lambda b, h, s: (0, 0)),
        ]

    return pl.pallas_call(
        functools.partial(_gla_kernel, nb=nb),
        grid=(batch, GLA_HEADS, nb),
        in_specs=specs(fwd, 0) + specs(bwd, 1),
        out_specs=[pl.BlockSpec((BLK, GLA_DV), lambda b, h, s: (fwd(b, h, s), h)),
                   pl.BlockSpec((BLK, GLA_DV), lambda b, h, s: (bwd(b, h, s), h))],
        out_shape=[jax.ShapeDtypeStruct((n, GLA_VW), BF16)] * 2,
        scratch_shapes=[pltpu.VMEM((GLA_DV, GLA_DK), F32), pltpu.VMEM((GLA_DV, GLA_DK), F32)],
        compiler_params=_params(("parallel", "parallel", "arbitrary")),
    )(proj, proj, proj, lr, gate_up, gate_bias, bd_f, proj, proj, proj, lr, gate_up, gate_bias, bd_b)


def _gated_head_rms(yf_ref, yb_ref, gate_ref, gain_ref, n_heads, act):
    y = yf_ref[...].astype(F32) + yb_ref[...].astype(F32)
    gate = act(gate_ref[...].astype(F32))
    gain = gain_ref[...]
    hd = y.shape[1] // n_heads
    outs = []
    for hh in range(n_heads):
        sl = slice(hh * hd, (hh + 1) * hd)
        a = y[:, sl]
        ms = jnp.mean(a * a, axis=-1, keepdims=True)
        outs.append((a * lax.rsqrt(ms + EPS) * gain[:, sl] * gate[:, sl]).astype(BF16))
    return jnp.concatenate(outs, axis=1)


def _out_proj0_kernel(ya_ref, yf_ref, yb_ref, gate_ref, gain_ref, wa_ref, wb_ref, h_ref, out_ref, *,
                      tiles_per_batch):
    tm = h_ref.shape[0]
    y_ml = _gated_head_rms(yf_ref, yb_ref, gate_ref, gain_ref, ML_HEADS, _sigmoid)
    out = (h_ref[...] + jnp.dot(ya_ref[...], wa_ref[...], preferred_element_type=F32)
           + jnp.dot(y_ml, wb_ref[...], preferred_element_type=F32))
    out_ref[...] = jnp.where(_row_valid(pl.program_id(0), tiles_per_batch, tm), out, 0.0)


def _out_proj1_kernel(yf_ref, yb_ref, gate_ref, gain_ref, w_ref, h_ref, out_ref, *, tiles_per_batch):
    tm = h_ref.shape[0]
    y = _gated_head_rms(yf_ref, yb_ref, gate_ref, gain_ref, GLA_HEADS, lambda r: r * _sigmoid(r))
    out = h_ref[...] + jnp.dot(y, w_ref[...], preferred_element_type=F32)
    out_ref[...] = jnp.where(_row_valid(pl.program_id(0), tiles_per_batch, tm), out, 0.0)


def _out_proj0(y_na, y_f, y_b, proj, gain, w_a, w_b, h, tp, tm):
    n, d = h.shape
    gate_blk = (3 * NA_WIDTH + 2 * ML_QK_WIDTH + ML_WIDTH) // ML_WIDTH
    row = lambda i: (i, 0)
    const = lambda i: (0, 0)
    return pl.pallas_call(
        functools.partial(_out_proj0_kernel, tiles_per_batch=tp // tm),
        grid=(n // tm,),
        in_specs=[
            pl.BlockSpec((tm, NA_WIDTH), row),
            pl.BlockSpec((tm, ML_WIDTH), row),
            pl.BlockSpec((tm, ML_WIDTH), row),
            pl.BlockSpec((tm, ML_WIDTH), lambda i: (i, gate_blk)),
            pl.BlockSpec((1, ML_WIDTH), const),
            pl.BlockSpec((NA_WIDTH, d), const),
            pl.BlockSpec((ML_WIDTH, d), const),
            pl.BlockSpec((tm, d), row),
        ],
        out_specs=pl.BlockSpec((tm, d), row),
        out_shape=jax.ShapeDtypeStruct((n, d), F32),
        compiler_params=_params(("parallel",)),
    )(y_na, y_f, y_b, proj, gain, w_a, w_b, h)


def _out_proj1(o_f, o_b, proj, gain, w, h, tp, tm):
    n, d = h.shape
    gate_blk = (2 * GLA_KW + GLA_VW) // GLA_VW
    row = lambda i: (i, 0)
    const = lambda i: (0, 0)
    return pl.pallas_call(
        functools.partial(_out_proj1_kernel, tiles_per_batch=tp // tm),
        grid=(n // tm,),
        in_specs=[
            pl.BlockSpec((tm, GLA_VW), row),
            pl.BlockSpec((tm, GLA_VW), row),
            pl.BlockSpec((tm, GLA_VW), lambda i: (i, gate_blk)),
            pl.BlockSpec((1, GLA_VW), const),
            pl.BlockSpec((GLA_VW, d), const),
            pl.BlockSpec((tm, d), row),
        ],
        out_specs=pl.BlockSpec((tm, d), row),
        out_shape=jax.ShapeDtypeStruct((n, d), F32),
        compiler_params=_params(("parallel",)),
    )(o_f, o_b, proj, gain, w, h)


def _ffn_kernel(h_ref, hp_ref, hn_ref, gain_ref, wg_ref, wv_ref, cwg_ref, cwv_ref, cbg_ref, cbv_ref,
                wd_ref, out_ref, xn_ref, *, tiles_per_batch, real_only, rc):
    i = pl.program_id(0)
    c = pl.program_id(1)
    tm = h_ref.shape[0]

    @pl.when(c == 0)
    def _():
        _rms_rows(h_ref, gain_ref, xn_ref, HALO, tm, rc)
        _rms_rows(hp_ref, gain_ref, xn_ref, 0, HALO, HALO)
        _rms_rows(hn_ref, gain_ref, xn_ref, HALO + tm, HALO, HALO)

        if not real_only:
            @pl.when(i == 0)
            def _():
                xn_ref[0:HALO, :] = jnp.zeros((HALO, xn_ref.shape[1]), xn_ref.dtype)

        @pl.when(i == pl.num_programs(0) - 1)
        def _():
            xn_ref[HALO + tm:, :] = jnp.zeros((HALO, xn_ref.shape[1]), xn_ref.dtype)

        out_ref[...] = h_ref[...]

    xn = xn_ref[...]

    def conv(w_ref, cw_ref, cb_ref):
        u = jnp.dot(xn, w_ref[...], preferred_element_type=F32)
        cw = cw_ref[...]
        out = cb_ref[...]
        for tap in range(CONV_W):
            off = HALO - CONV_W // 2 + tap
            out = out + u[off:off + tm] * cw[tap:tap + 1]
        return out

    g = conv(wg_ref, cwg_ref, cbg_ref)
    val = conv(wv_ref, cwv_ref, cbv_ref)
    act = (g * _sigmoid(g) * val).astype(BF16)
    out_ref[...] += jnp.dot(act, wd_ref[...], preferred_element_type=F32)

    if not real_only:
        @pl.when(jnp.logical_and(c == pl.num_programs(1) - 1, i % tiles_per_batch == 0))
        def _():
            out_ref[0:PAD, :] = jnp.zeros((PAD, out_ref.shape[1]), F32)


def _ffn(h, gain, w_up, conv_w, conv_b, w_down, tp, tm, tf, real_only):
    n, d = h.shape
    d_ff = w_down.shape[0]
    n_ff = d_ff // tf
    if real_only:
        tiles_per_batch = (tp - BLK) // tm
        n_out = n // tp * (tp - BLK)
        row0 = lambda i: (i // tiles_per_batch) * tp + BLK + (i % tiles_per_batch) * tm
        window = lambda rows, start: pl.BlockSpec((pl.Element(rows), pl.Element(d)),
                                                  lambda i, c: (pl.multiple_of(start(i), HALO), 0))
        h_specs = [window(tm, row0), window(HALO, lambda i: row0(i) - HALO),
                   window(HALO, lambda i: jnp.minimum(row0(i) + tm, n - HALO))]
    else:
        tiles_per_batch = tp // tm
        n_out = n
        hb = tm // HALO
        last_halo = n // HALO - 1
        h_specs = [pl.BlockSpec((tm, d), lambda i, c: (i, 0)),
                   pl.BlockSpec((HALO, d), lambda i, c: (jnp.maximum(i * hb - 1, 0), 0)),
                   pl.BlockSpec((HALO, d), lambda i, c: (jnp.minimum((i + 1) * hb, last_halo), 0))]
    return pl.pallas_call(
        functools.partial(_ffn_kernel, tiles_per_batch=tiles_per_batch, real_only=real_only,
                          rc=_pick_tile(tm, (128, 64))),
        grid=(n_out // tm, n_ff),
        in_specs=h_specs + [
            pl.BlockSpec((1, d), lambda i, c: (0, 0)),
            pl.BlockSpec((d, tf), lambda i, c: (0, c)),
            pl.BlockSpec((d, tf), lambda i, c: (0, n_ff + c)),
            pl.BlockSpec((CONV_W, tf), lambda i, c: (0, c)),
            pl.BlockSpec((CONV_W, tf), lambda i, c: (0, n_ff + c)),
            pl.BlockSpec((1, tf), lambda i, c: (0, c)),
            pl.BlockSpec((1, tf), lambda i, c: (0, n_ff + c)),
            pl.BlockSpec((tf, d), lambda i, c: (c, 0)),
        ],
        out_specs=pl.BlockSpec((tm, d), lambda i, c: (i, 0)),
        out_shape=jax.ShapeDtypeStruct((n_out, d), F32),
        scratch_shapes=[pltpu.VMEM((tm + 2 * HALO, d), BF16)],
        compiler_params=_params(("parallel", "arbitrary")),
    )(h, h, h, gain, w_up, w_up, conv_w, conv_w, conv_b, conv_b, w_down)


def kernel(x, meta_tokens, norm_mix, norm_ffn, ab_w_in, ab_gate_bias, ab_q_gain, ab_k_gain, ab_rel_bias, ab_ml_gain, ab_w_out, c_w_in, c_gate_up, c_gate_bias, c_head_gain, c_w_out, ffn_w_up, ffn_conv_w, ffn_conv_b, ffn_w_down):
    batch, seq, d = x.shape
    d_ff = ffn_w_down.shape[1]
    assert seq % (NA_GROUP * GRID_W) == 0 and seq // GRID_W >= NA_UNION and d % LANES == 0
    tp = BLK + seq
    nb = tp // BLK
    n = batch * tp
    tm_proj = _pick_tile(tp, (1280, 1024, 768, 512, 256))
    tm_out = _pick_tile(tp, (512, 256))
    tm_ffn = _pick_tile(tp, (832, 640, 512, 256))
    tm_ffn_real = _pick_tile(seq, (1024, 512, 256))
    tn_proj = 1024
    tf = _pick_tile(d_ff, (512, 256, 128))

    meta = jnp.broadcast_to(meta_tokens.astype(x.dtype)[None], (batch, N_META, d))
    h = jnp.concatenate([jnp.zeros((batch, PAD, d), x.dtype), meta, x], axis=1).reshape(n, d)

    n_main0 = 3 * NA_WIDTH + 2 * ML_QK_WIDTH + 2 * ML_WIDTH
    w_in = ab_w_in[0]
    ones = lambda k: jnp.ones((k,), F32)
    colscale0 = jnp.concatenate([
        jnp.tile(ab_q_gain[0].astype(F32), NA_HEADS), jnp.tile(ab_k_gain[0].astype(F32), NA_HEADS),
        ones(NA_WIDTH), ones(ML_QK_WIDTH), jnp.full((ML_QK_WIDTH,), ML_QK ** -0.5, F32),
        ones(2 * ML_WIDTH)])[None]
    proj0, graw = _in_proj(h, norm_mix[0][None].astype(F32), w_in[:, :n_main0].astype(BF16), colscale0,
                           w_in[:, n_main0:].T.astype(BF16), n_rms=2 * NA_WIDTH // tn_proj, tail_nt=True,
                           tm=tm_proj, tn=tn_proj)
    gr, gc = _mlstm_gates(graw, ab_gate_bias[0].astype(F32)[:, None], nb)
    y_na = _na_attention(proj0, _na_bias_tables(ab_rel_bias[0], seq // GRID_W), batch, tp)
    y_f, y_b = _mlstm(proj0, gr, gc, batch, nb)
    w_out = ab_w_out[0].astype(BF16)
    h = _out_proj0(y_na, y_f, y_b, proj0, ab_ml_gain[0][None].astype(F32), w_out[:NA_WIDTH], w_out[NA_WIDTH:],
                   h, tp, tm_out)
    h = _ffn(h, norm_ffn[0][None].astype(F32), ffn_w_up[0].astype(BF16), ffn_conv_w[0].astype(F32),
             ffn_conv_b[0][None].astype(F32), ffn_w_down[0].astype(BF16), tp, tm_ffn, tf, False)

    n_main1 = 2 * GLA_KW + 2 * GLA_VW
    w_in = c_w_in[0]
    colscale1 = jnp.concatenate([jnp.full((GLA_KW,), GLA_DK ** -0.5, F32), ones(n_main1 - GLA_KW)])[None]
    w_tail = jnp.pad(w_in[:, n_main1:], ((0, 0), (0, LANES - 2 * GLA_RANK))).astype(BF16)
    proj1, lr = _in_proj(h, norm_mix[1][None].astype(F32), w_in[:, :n_main1].astype(BF16), colscale1, w_tail,
                         n_rms=0, tail_nt=False, tm=tm_proj, tn=tn_proj)
    gu = c_gate_up[0].astype(BF16).reshape(2, GLA_RANK, GLA_HEADS, GLA_DK).transpose(0, 2, 1, 3)
    gate_up = jnp.stack([jnp.pad(gu[0], ((0, 0), (0, LANES - GLA_RANK), (0, 0))),
                         jnp.pad(gu[1], ((0, 0), (GLA_RANK, LANES - 2 * GLA_RANK), (0, 0)))])
    gate_bias = c_gate_bias[0].astype(F32).reshape(2, GLA_HEADS, 1, GLA_DK)
    o_f, o_b = _gla(proj1, lr, gate_up, gate_bias, batch, nb)
    h = _out_proj1(o_f, o_b, proj1, c_head_gain[0][None].astype(F32), c_w_out[0].astype(BF16), h, tp, tm_out)
    out = _ffn(h, norm_ffn[1][None].astype(F32), ffn_w_up[1].astype(BF16), ffn_conv_w[1].astype(F32),
               ffn_conv_b[1][None].astype(F32), ffn_w_down[1].astype(BF16), tp, tm_ffn_real, tf, True)
    return out.reshape(batch, seq, d)
```

```python
import functools

import numpy as np
import jax
import jax.numpy as jnp
from jax import lax
from jax.experimental import pallas as pl
from jax.experimental.pallas import tpu as pltpu

F32 = jnp.float32
BF16 = jnp.bfloat16
EPS = 1e-6

N_META = 16
GRID_W = 64
NA_HEADS = 8
NA_DIM = 128
NA_WIDTH = NA_HEADS * NA_DIM
WIN_H = 8
WIN_W = 16
ML_HEADS = 4
ML_QK = 128
ML_V = 256
ML_QK_WIDTH = ML_HEADS * ML_QK
ML_WIDTH = ML_HEADS * ML_V
N_ML_GATES = 4 * ML_HEADS
GLA_HEADS = 4
GLA_DK = 256
GLA_DV = 512
GLA_KW = GLA_HEADS * GLA_DK
GLA_VW = GLA_HEADS * GLA_DV
GLA_RANK = 16
GLA_TAU = 16.0
GLA_CHUNK = 64
CONV_W = 3

LANES = 128
BLK = 256
PAD = BLK - N_META
NA_GROUP = 4
NA_UNION = NA_GROUP + WIN_H - 1
HALO = 16
FFN_SUBTILES = 2
VMEM_LIMIT = 56 * 1024 * 1024

_NT = (((1,), (1,)), ((), ()))
_TN = (((0,), (0,)), ((), ()))


def _log_sigmoid(x):
    return jnp.minimum(x, 0.0) - jnp.log(1.0 + jnp.exp(-jnp.abs(x)))


def _sigmoid(x):
    return 1.0 / (1.0 + jnp.exp(-x))


def _pick_tile(n, prefs):
    for t in prefs:
        if n % t == 0:
            return t
    raise ValueError(f"no tile in {prefs} divides {n}")


def _params(sem, vmem=VMEM_LIMIT):
    return pltpu.CompilerParams(dimension_semantics=sem, vmem_limit_bytes=vmem)


def _rms_rows(x_ref, gain_ref, dst_ref, dst_row0, n_rows, rc):
    g = gain_ref[...]

    def body(r, carry):
        r0 = pl.multiple_of(r * rc, rc)
        xx = x_ref[pl.ds(r0, rc), :]
        ms = jnp.mean(xx * xx, axis=-1, keepdims=True)
        dst_ref[pl.ds(dst_row0 + r0, rc), :] = (xx * lax.rsqrt(ms + EPS) * g).astype(dst_ref.dtype)
        return carry

    lax.fori_loop(0, n_rows // rc, body, 0)


def _row_valid(tile_idx, tiles_per_batch, tm):
    it = tile_idx % tiles_per_batch
    row = lax.broadcasted_iota(jnp.int32, (tm, 1), 0)
    return (row + it * tm) >= PAD


def _in_proj_kernel(x_ref, gain_ref, w_ref, cs_ref, wt_ref, out_ref, tail_ref, xn_ref, *,
                    n_rms, tail_nt, rc):
    j = pl.program_id(1)
    tm = x_ref.shape[0]
    tn = w_ref.shape[1]

    @pl.when(j == 0)
    def _():
        _rms_rows(x_ref, gain_ref, xn_ref, 0, tm, rc)
        if tail_nt:
            tail_ref[...] = lax.dot_general(wt_ref[...], xn_ref[...], _NT, preferred_element_type=F32)
        else:
            tail_ref[...] = jnp.dot(xn_ref[...], wt_ref[...],
                                    preferred_element_type=F32).astype(tail_ref.dtype)

    @pl.when(j > 0)
    def _():
        acc = jnp.dot(xn_ref[...], w_ref[...], preferred_element_type=F32)
        cs = cs_ref[...]
        if n_rms > 0:
            @pl.when(j <= n_rms)
            def _():
                for hh in range(tn // LANES):
                    sl = slice(hh * LANES, (hh + 1) * LANES)
                    a = acc[:, sl]
                    ms = jnp.mean(a * a, axis=-1, keepdims=True)
                    out_ref[:, sl] = (a * lax.rsqrt(ms + EPS) * cs[:, sl]).astype(out_ref.dtype)

            @pl.when(j > n_rms)
            def _():
                out_ref[...] = (acc * cs).astype(out_ref.dtype)
        else:
            out_ref[...] = (acc * cs).astype(out_ref.dtype)


def _in_proj(h, gain, w, colscale, w_tail, *, n_rms, tail_nt, tm, tn):
    n, d = h.shape
    width_main = colscale.shape[1]
    n_main = width_main // tn
    if tail_nt:
        n_tail = w_tail.shape[0]
        wt_spec = pl.BlockSpec((n_tail, d), lambda i, j: (0, 0))
        tail_shape = jax.ShapeDtypeStruct((n_tail, n), F32)
        tail_spec = pl.BlockSpec((n_tail, tm), lambda i, j: (0, i))
    else:
        n_tail = w_tail.shape[1]
        wt_spec = pl.BlockSpec((d, n_tail), lambda i, j: (0, 0))
        tail_shape = jax.ShapeDtypeStruct((n, n_tail), BF16)
        tail_spec = pl.BlockSpec((tm, n_tail), lambda i, j: (i, 0))
    col = lambda i, j: (0, jnp.maximum(j - 1, 0))
    kern = functools.partial(_in_proj_kernel, n_rms=n_rms, tail_nt=tail_nt, rc=128)
    return pl.pallas_call(
        kern,
        grid=(n // tm, n_main + 1),
        in_specs=[
            pl.BlockSpec((tm, d), lambda i, j: (i, 0)),
            pl.BlockSpec((1, d), lambda i, j: (0, 0)),
            pl.BlockSpec((d, tn), col),
            pl.BlockSpec((1, tn), col),
            wt_spec,
        ],
        out_specs=[
            pl.BlockSpec((tm, tn), lambda i, j: (i, jnp.maximum(j - 1, 0))),
            tail_spec,
        ],
        out_shape=[jax.ShapeDtypeStruct((n, width_main), BF16), tail_shape],
        scratch_shapes=[pltpu.VMEM((tm, d), BF16)],
        compiler_params=_params(("parallel", "arbitrary")),
    )(h, gain, w, colscale, w_tail)


def _mlstm_gates_kernel(graw_ref, bias_ref, gr_ref, gc_ref, *, tiles_per_batch):
    width = graw_ref.shape[1]
    g = graw_ref[...] + bias_ref[...]
    row = lax.broadcasted_iota(jnp.int32, g.shape, 0)
    pos = lax.broadcasted_iota(jnp.int32, g.shape, 1)
    lane = pos & (BLK - 1)
    is_forget = (row & 4) != 0
    valid = (pos + (pl.program_id(0) % tiles_per_batch) * width) >= PAD
    lf = jnp.where(valid, _log_sigmoid(g), 0.0)
    pre = lf
    suf = lf
    d = 1
    while d < BLK:
        pre = pre + jnp.where(lane >= d, pltpu.roll(pre, d, axis=1), 0.0)
        suf = suf + jnp.where(lane < BLK - d, pltpu.roll(suf, width - d, axis=1), 0.0)
        d *= 2
    ig = jnp.where(valid, g, -jnp.inf)
    out = jnp.where(is_forget, jnp.where(row >= 8, suf, pre), ig)
    a = out - pltpu.roll(out, N_ML_GATES - 4, axis=0)
    pmax = a
    smax = a
    d = 1
    while d < BLK:
        pmax = jnp.maximum(pmax, jnp.where(lane >= d, pltpu.roll(pmax, d, axis=1), -jnp.inf))
        smax = jnp.maximum(smax, jnp.where(lane < BLK - d, pltpu.roll(smax, width - d, axis=1), -jnp.inf))
        d *= 2
    runmax = jnp.where(is_forget, 0.0, jnp.where(row >= 8, smax, pmax))
    both = jnp.concatenate([out, runmax], axis=0)
    gr_ref[...] = both
    full = jnp.concatenate([both, jnp.zeros((LANES - 2 * N_ML_GATES, width), F32)], axis=0)
    gc_ref[...] = full.T


def _mlstm_gates(graw, bias, tp, width):
    n = graw.shape[1]
    return pl.pallas_call(
        functools.partial(_mlstm_gates_kernel, tiles_per_batch=tp // width),
        grid=(n // width,),
        in_specs=[pl.BlockSpec((N_ML_GATES, width), lambda i: (0, i)),
                  pl.BlockSpec((N_ML_GATES, 1), lambda i: (0, 0))],
        out_specs=[pl.BlockSpec((2 * N_ML_GATES, width), lambda i: (0, i)),
                   pl.BlockSpec((width, LANES), lambda i: (i, 0))],
        out_shape=[jax.ShapeDtypeStruct((2 * N_ML_GATES, n), F32),
                   jax.ShapeDtypeStruct((n, LANES), F32)],
        compiler_params=_params(("parallel",)),
    )(graw, bias)


def _na_bias_tables(rel_bias, rows):
    a = np.arange(NA_GROUP)[:, None]
    u = np.arange(NA_UNION)[None, :]
    half = WIN_H // 2
    vis_first = (u < WIN_H) & (a >= 0)
    dr_first = u - a + WIN_H - 1
    vis_mid = (u >= a) & (u < a + WIN_H)
    dr_mid = u - a + WIN_H - 1 - half
    lo = NA_UNION - WIN_H
    vis_last = (u >= lo) & (a >= 0)
    dr_last = u - a + (WIN_H - 1) - (NA_UNION - NA_GROUP)
    vis_r = np.stack([vis_first, vis_mid, vis_last])
    dr = np.clip(np.stack([dr_first, dr_mid, dr_last]), 0, 2 * WIN_H - 2)
    qc = np.arange(GRID_W)[:, None]
    kc = np.arange(GRID_W)[None, :]
    c0 = np.clip(qc - WIN_W // 2, 0, GRID_W - WIN_W)
    vis_c = (kc >= c0) & (kc < c0 + WIN_W)
    dc = np.clip(kc - qc + WIN_W - 1, 0, 2 * WIN_W - 2)
    sel_c = (dc[..., None] == np.arange(2 * WIN_W - 1)).astype(np.float32)
    cols = jnp.einsum('hrd,qkd->hrqk', rel_bias.astype(F32), sel_c, precision=lax.Precision.HIGHEST)
    cols = jnp.where(vis_c, cols, -jnp.inf)
    hidden = jnp.full((NA_HEADS, GRID_W, GRID_W), -jnp.inf, F32)
    return jnp.stack([
        jnp.concatenate([
            jnp.concatenate([cols[:, dr[c, qa, ku]] if vis_r[c, qa, ku] else hidden
                             for ku in range(NA_UNION)], axis=2)
            for qa in range(NA_GROUP)], axis=1)
        for c in range(3)])


def _na_kernel(q_ref, k_ref, v_ref, bias_ref, out_ref, *, rows):
    n_groups = rows // NA_GROUP
    gq = NA_GROUP * GRID_W
    uk = NA_UNION * GRID_W
    scale = NA_DIM ** -0.5
    km = k_ref[PAD:BLK, :]
    vm = v_ref[PAD:BLK, :]

    def attend(q, kw, vw, bias):
        s_m = lax.dot_general(q, km, _NT, preferred_element_type=F32) * scale
        m = jnp.max(s_m, axis=-1, keepdims=True)
        if kw is not None:
            s = lax.dot_general(q, kw, _NT, preferred_element_type=F32) * scale + bias
            m = jnp.maximum(m, jnp.max(s, axis=-1, keepdims=True))
        p_m = jnp.exp(s_m - m)
        den = jnp.sum(p_m, axis=-1, keepdims=True)
        y = jnp.dot(p_m.astype(BF16), vm, preferred_element_type=F32)
        if kw is not None:
            p = jnp.exp(s - m)
            den = den + jnp.sum(p, axis=-1, keepdims=True)
            y = y + jnp.dot(p.astype(BF16), vw, preferred_element_type=F32)
        return y / den

    out_ref[0:PAD, :] = jnp.zeros((PAD, NA_DIM), out_ref.dtype)
    out_ref[PAD:BLK, :] = attend(q_ref[PAD:BLK, :], None, None, None).astype(out_ref.dtype)

    def group(g):
        q0 = pl.multiple_of(BLK + g * gq, gq)
        u0 = jnp.clip(g * NA_GROUP - WIN_H // 2, 0, rows - NA_UNION)
        k0 = pl.multiple_of(BLK + u0 * GRID_W, GRID_W)
        cls = jnp.where(g == 0, 0, jnp.where(g == n_groups - 1, 2, 1))
        return attend(q_ref[pl.ds(q0, gq), :], k_ref[pl.ds(k0, uk), :], v_ref[pl.ds(k0, uk), :],
                      bias_ref[cls])

    def body(gg, carry):
        g = gg * 2
        y = jnp.concatenate([group(g), group(g + 1)], axis=0)
        out_ref[pl.ds(pl.multiple_of(BLK + g * gq, gq), 2 * gq), :] = y.astype(out_ref.dtype)
        return carry

    lax.fori_loop(0, n_groups // 2, body, 0)


def _na_attention(proj, bias, batch, tp):
    rows = (tp - BLK) // GRID_W
    n = proj.shape[0]
    gq = NA_GROUP * GRID_W
    uk = NA_UNION * GRID_W
    return pl.pallas_call(
        functools.partial(_na_kernel, rows=rows),
        grid=(batch, NA_HEADS),
        in_specs=[
            pl.BlockSpec((tp, NA_DIM), lambda b, h: (b, h)),
            pl.BlockSpec((tp, NA_DIM), lambda b, h: (b, NA_HEADS + h)),
            pl.BlockSpec((tp, NA_DIM), lambda b, h: (b, 2 * NA_HEADS + h)),
            pl.BlockSpec((3, None, gq, uk), lambda b, h: (0, h, 0, 0)),
        ],
        out_specs=pl.BlockSpec((tp, NA_DIM), lambda b, h: (b, h)),
        out_shape=jax.ShapeDtypeStruct((n, NA_WIDTH), BF16),
        compiler_params=_params(("parallel", "parallel")),
    )(proj, proj, proj, bias)


def _mlstm_dir(q_ref, k_ref, v_ref, gr_ref, gc_ref, y_ref, c_ref, m_ref, head, rev):
    q = q_ref[:, head * ML_QK:(head + 1) * ML_QK]
    k = k_ref[:, head * ML_QK:(head + 1) * ML_QK]
    v = v_ref[:, head * ML_V:(head + 1) * ML_V]
    base = 8 if rev else 0
    a_r = gr_ref[base + head:base + head + 1, :] - gr_ref[base + 4 + head:base + 5 + head, :]
    a_c = gc_ref[:, base + head:base + head + 1] - gc_ref[:, base + 4 + head:base + 5 + head]
    b_c = gc_ref[:, base + 4 + head:base + 5 + head]
    run_c = gc_ref[:, N_ML_GATES + base + head:N_ML_GATES + base + head + 1]
    m = m_ref[head]
    m_rel = jnp.broadcast_to(jnp.maximum(m, run_c), (BLK, LANES))
    ti = lax.broadcasted_iota(jnp.int32, (BLK, BLK), 0)
    si = lax.broadcasted_iota(jnp.int32, (BLK, BLK), 1)
    mask = (si >= ti) if rev else (si <= ti)
    arg = jnp.where(mask, a_r - jnp.concatenate([m_rel] * (BLK // LANES), axis=1), -jnp.inf)
    p = jnp.exp(arg) * lax.dot_general(q, k, _NT, preferred_element_type=F32)
    w_prev = jnp.exp(m - m_rel)
    v_ext = jnp.concatenate([v, jnp.ones((BLK, LANES), BF16)], axis=1)
    c_ext = c_ref[head]
    num = (jnp.concatenate([w_prev] * (c_ext.shape[1] // LANES), axis=1)
           * jnp.dot(q, c_ext.astype(BF16), preferred_element_type=F32)
           + jnp.dot(p.astype(BF16), v_ext, preferred_element_type=F32))
    den = num[:, ML_V:]
    inv = 1.0 / jnp.maximum(jnp.abs(den), jnp.exp(-(jnp.broadcast_to(b_c, (BLK, LANES)) + m_rel)))
    y_ref[:, head * ML_V:(head + 1) * ML_V] = (
        num[:, :ML_V] * jnp.concatenate([inv] * (ML_V // LANES), axis=1)).astype(y_ref.dtype)
    last = 0 if rev else BLK - 1
    b_end = gr_ref[base + 4 + head:base + 5 + head, last:last + 1]
    m_end = jnp.maximum(m, gr_ref[N_ML_GATES + base + head:N_ML_GATES + base + head + 1, last:last + 1])
    w_tok = jnp.exp(a_c - m_end)
    c_ref[head] = jnp.exp(m - m_end) * c_ext + lax.dot_general(
        k, (w_tok * v_ext.astype(F32)).astype(BF16), _TN, preferred_element_type=F32)
    m_ref[head] = b_end + m_end


def _mlstm_kernel(qf, kf, vf, grf, gcf, qb, kb, vb, grb, gcb, yf, yb, cf, mf, cb, mb):
    @pl.when(pl.program_id(1) == 0)
    def _():
        cf[...] = jnp.zeros(cf.shape, F32)
        cb[...] = jnp.zeros(cb.shape, F32)
        mf[...] = jnp.zeros(mf.shape, F32)
        mb[...] = jnp.zeros(mb.shape, F32)

    for head in range(ML_HEADS):
        _mlstm_dir(qf, kf, vf, grf, gcf, yf, cf, mf, head, False)
        _mlstm_dir(qb, kb, vb, grb, gcb, yb, cb, mb, head, True)


def _mlstm(proj, gr, gc, batch, nb):
    n = proj.shape[0]
    q0 = 3 * NA_WIDTH // ML_QK_WIDTH
    k0 = q0 + 1
    v0 = (3 * NA_WIDTH + 2 * ML_QK_WIDTH) // ML_WIDTH
    fwd = lambda b, s: b * nb + s
    bwd = lambda b, s: b * nb + nb - 1 - s

    def specs(pos):
        return [
            pl.BlockSpec((BLK, ML_QK_WIDTH), lambda b, s: (pos(b, s), q0)),
            pl.BlockSpec((BLK, ML_QK_WIDTH), lambda b, s: (pos(b, s), k0)),
            pl.BlockSpec((BLK, ML_WIDTH), lambda b, s: (pos(b, s), v0)),
            pl.BlockSpec((2 * N_ML_GATES, BLK), lambda b, s: (0, pos(b, s))),
            pl.BlockSpec((BLK, LANES), lambda b, s: (pos(b, s), 0)),
        ]

    state = [pltpu.VMEM((ML_HEADS, ML_QK, ML_V + LANES), F32), pltpu.VMEM((ML_HEADS, 1, 1), F32)]
    return pl.pallas_call(
        _mlstm_kernel,
        grid=(batch, nb),
        in_specs=specs(fwd) + specs(bwd),
        out_specs=[pl.BlockSpec((BLK, ML_WIDTH), lambda b, s: (fwd(b, s), 0)),
                   pl.BlockSpec((BLK, ML_WIDTH), lambda b, s: (bwd(b, s), 0))],
        out_shape=[jax.ShapeDtypeStruct((n, ML_WIDTH), BF16)] * 2,
        scratch_shapes=state + state,
        compiler_params=_params(("parallel", "arbitrary")),
    )(proj, proj, proj, gr, gc, proj, proj, proj, gr, gc)


def _gla_dir(q_ref, k_ref, v_ref, lr_ref, gu_ref, gb_ref, tri_ref, o_ref, st_ref, valid, rev):
    dk = q_ref.shape[1]
    z = jnp.dot(lr_ref[...], gu_ref[...], preferred_element_type=F32) + gb_ref[...]
    la = jnp.where(valid, _log_sigmoid(z) * (1.0 / GLA_TAU), 0.0)
    t1 = la.astype(BF16)
    r1 = la - t1.astype(F32)
    t2 = r1.astype(BF16)
    t3 = (r1 - t2.astype(F32)).astype(BF16)
    tri = tri_ref[...]
    b = (jnp.dot(tri, t1, preferred_element_type=F32) + jnp.dot(tri, t2, preferred_element_type=F32)
         + jnp.dot(tri, t3, preferred_element_type=F32))
    q = q_ref[...].astype(F32)
    k = k_ref[...].astype(F32)
    b_end = b[0:1] if rev else b[BLK - 1:BLK]
    q_in = (q * jnp.exp(b)).astype(BF16)
    k_dec = (k * jnp.exp(b_end - b)).astype(BF16)
    sub = GLA_CHUNK
    ti = lax.broadcasted_iota(jnp.int32, (sub, BLK), 0)
    si = lax.broadcasted_iota(jnp.int32, (sub, BLK), 1)

    def at_rows(x, row0):
        parts = [jnp.zeros((n, dk), BF16) for n in (row0,) if n] + [x]
        after = BLK - row0 - x.shape[0]
        if after:
            parts.append(jnp.zeros((after, dk), BF16))
        return jnp.concatenate(parts, axis=0) if len(parts) > 1 else x

    a_rows = []
    for i in range(BLK // sub):
        lo, hi = i * sub, (i + 1) * sub
        b_i, q_i = b[lo:hi], q[lo:hi]
        mu = b[lo + sub // 2:lo + sub // 2 + 1]
        lhs = (q_i * jnp.exp(b_i - mu)).astype(BF16)
        rhs = at_rows((k[lo:hi] * jnp.exp(mu - b_i)).astype(BF16), lo)
        n_prev = BLK - hi if rev else lo
        if n_prev:
            prev = slice(hi, BLK) if rev else slice(0, lo)
            rho = b[hi:hi + 1] if rev else b[lo - 1:lo]
            k_prev = at_rows((k[prev] * jnp.exp(rho - b[prev])).astype(BF16), hi if rev else 0)
            lhs = jnp.concatenate([(q_i * jnp.exp(b_i - rho)).astype(BF16), lhs], axis=1)
            rhs = jnp.concatenate([k_prev, rhs], axis=1)
        a_i = lax.dot_general(lhs, rhs, _NT, preferred_element_type=F32)
        seen = (si >= ti + lo) if rev else (si <= ti + lo)
        a_rows.append(jnp.where(seen, a_i, 0.0).astype(BF16))
    a = jnp.concatenate(a_rows, axis=0)
    st = st_ref[...]
    v = v_ref[...]
    o_ref[...] = (lax.dot_general(q_in, st.astype(BF16), _NT, preferred_element_type=F32)
                  + jnp.dot(a, v, preferred_element_type=F32)).astype(o_ref.dtype)
    st_ref[...] = st * jnp.exp(b_end) + lax.dot_general(v, k_dec, _TN, preferred_element_type=F32)


def _gla_kernel(qf, kf, vf, lrf, guf, gbf, bdf, qb, kb, vb, lrb, gub, gbb, bdb, of, ob, sf, sb, *, nb):
    s = pl.program_id(2)

    @pl.when(s == 0)
    def _():
        sf[...] = jnp.zeros(sf.shape, F32)
        sb[...] = jnp.zeros(sb.shape, F32)

    row = lax.broadcasted_iota(jnp.int32, (BLK, 1), 0)
    _gla_dir(qf, kf, vf, lrf, guf, gbf, bdf, of, sf, (row + s * BLK) >= PAD, False)
    _gla_dir(qb, kb, vb, lrb, gub, gbb, bdb, ob, sb, (row + (nb - 1 - s) * BLK) >= PAD, True)


def _gla(proj, lr, gate_up, gate_bias, batch, nb):
    n = proj.shape[0]
    k0 = GLA_KW // GLA_DK
    v0 = 2 * GLA_KW // GLA_DV
    fwd = lambda b, h, s: b * nb + s
    bwd = lambda b, h, s: b * nb + nb - 1 - s
    t = np.arange(BLK)
    bd_f = jnp.asarray(t[None, :] <= t[:, None], BF16)
    bd_b = jnp.asarray(t[None, :] >= t[:, None], BF16)

    def specs(pos, direction):
        return [
            pl.BlockSpec((BLK, GLA_DK), lambda b, h, s: (pos(b, h, s), h)),
            pl.BlockSpec((BLK, GLA_DK), lambda b, h, s: (pos(b, h, s), k0 + h)),
            pl.BlockSpec((BLK, GLA_DV), lambda b, h, s: (pos(b, h, s), v0 + h)),
            pl.BlockSpec((BLK, LANES), lambda b, h, s: (pos(b, h, s), 0)),
            pl.BlockSpec((None, None, LANES, GLA_DK), lambda b, h, s: (direction, h, 0, 0)),
            pl.BlockSpec((None, None, 1, GLA_DK), lambda b, h, s: (direction, h, 0, 0)),
            pl.BlockSpec((BLK, BLK), lambda b, h, s: (0, 0)),
        ]

    return pl.pallas_call(
        functools.partial(_gla_kernel, nb=nb),
        grid=(batch, GLA_HEADS, nb),
        in_specs=specs(fwd, 0) + specs(bwd, 1),
        out_specs=[pl.BlockSpec((BLK, GLA_DV), lambda b, h, s: (fwd(b, h, s), h)),
                   pl.BlockSpec((BLK, GLA_DV), lambda b, h, s: (bwd(b, h, s), h))],
        out_shape=[jax.ShapeDtypeStruct((n, GLA_VW), BF16)] * 2,
        scratch_shapes=[pltpu.VMEM((GLA_DV, GLA_DK), F32), pltpu.VMEM((GLA_DV, GLA_DK), F32)],
        compiler_params=_params(("parallel", "parallel", "arbitrary")),
    )(proj, proj, proj, lr, gate_up, gate_bias, bd_f, proj, proj, proj, lr, gate_up, gate_bias, bd_b)


def _gated_head_rms(yf_ref, yb_ref, gate_ref, gain_ref, n_heads, act):
    y = yf_ref[...].astype(F32) + yb_ref[...].astype(F32)
    gate = act(gate_ref[...].astype(F32))
    gain = gain_ref[...]
    hd = y.shape[1] // n_heads
    outs = []
    for hh in range(n_heads):
        sl = slice(hh * hd, (hh + 1) * hd)
        a = y[:, sl]
        ms = jnp.mean(a * a, axis=-1, keepdims=True)
        outs.append((a * lax.rsqrt(ms + EPS) * gain[:, sl] * gate[:, sl]).astype(BF16))
    return jnp.concatenate(outs, axis=1)


def _out_proj0_kernel(ya_ref, yf_ref, yb_ref, gate_ref, gain_ref, wa_ref, wb_ref, h_ref, out_ref, *,
                      tiles_per_batch):
    tm = h_ref.shape[0]
    y_ml = _gated_head_rms(yf_ref, yb_ref, gate_ref, gain_ref, ML_HEADS, _sigmoid)
    out = (h_ref[...] + jnp.dot(ya_ref[...], wa_ref[...], preferred_element_type=F32)
           + jnp.dot(y_ml, wb_ref[...], preferred_element_type=F32))
    out_ref[...] = jnp.where(_row_valid(pl.program_id(0), tiles_per_batch, tm), out, 0.0)


def _out_proj1_kernel(yf_ref, yb_ref, gate_ref, gain_ref, w_ref, h_ref, out_ref, *, tiles_per_batch):
    tm = h_ref.shape[0]
    y = _gated_head_rms(yf_ref, yb_ref, gate_ref, gain_ref, GLA_HEADS, lambda r: r * _sigmoid(r))
    out = h_ref[...] + jnp.dot(y, w_ref[...], preferred_element_type=F32)
    out_ref[...] = jnp.where(_row_valid(pl.program_id(0), tiles_per_batch, tm), out, 0.0)


def _out_proj0(y_na, y_f, y_b, proj, gain, w, h, tp, tm):
    n, d = h.shape
    assert NA_WIDTH == ML_WIDTH
    gate_blk = (3 * NA_WIDTH + 2 * ML_QK_WIDTH + ML_WIDTH) // ML_WIDTH
    row = lambda i: (i, 0)
    const = lambda i: (0, 0)
    return pl.pallas_call(
        functools.partial(_out_proj0_kernel, tiles_per_batch=tp // tm),
        grid=(n // tm,),
        in_specs=[
            pl.BlockSpec((tm, NA_WIDTH), row),
            pl.BlockSpec((tm, ML_WIDTH), row),
            pl.BlockSpec((tm, ML_WIDTH), row),
            pl.BlockSpec((tm, ML_WIDTH), lambda i: (i, gate_blk)),
            pl.BlockSpec((1, ML_WIDTH), const),
            pl.BlockSpec((NA_WIDTH, d), const, pipeline_mode=pl.Buffered(1)),
            pl.BlockSpec((ML_WIDTH, d), lambda i: (1, 0), pipeline_mode=pl.Buffered(1)),
            pl.BlockSpec((tm, d), row),
        ],
        out_specs=pl.BlockSpec((tm, d), row),
        out_shape=jax.ShapeDtypeStruct((n, d), F32),
        compiler_params=_params(("parallel",)),
    )(y_na, y_f, y_b, proj, gain, w, w, h)


def _out_proj1(o_f, o_b, proj, gain, w, h, tp, tm):
    n, d = h.shape
    gate_blk = (2 * GLA_KW + GLA_VW) // GLA_VW
    row = lambda i: (i, 0)
    const = lambda i: (0, 0)
    return pl.pallas_call(
        functools.partial(_out_proj1_kernel, tiles_per_batch=tp // tm),
        grid=(n // tm,),
        in_specs=[
            pl.BlockSpec((tm, GLA_VW), row),
            pl.BlockSpec((tm, GLA_VW), row),
            pl.BlockSpec((tm, GLA_VW), lambda i: (i, gate_blk)),
            pl.BlockSpec((1, GLA_VW), const),
            pl.BlockSpec((GLA_VW, d), const, pipeline_mode=pl.Buffered(1)),
            pl.BlockSpec((tm, d), row),
        ],
        out_specs=pl.BlockSpec((tm, d), row),
        out_shape=jax.ShapeDtypeStruct((n, d), F32),
        compiler_params=_params(("parallel",)),
    )(o_f, o_b, proj, gain, w, h)


def _ffn_kernel(h_ref, hp_ref, hn_ref, gain_ref, wg_ref, wv_ref, cwg_ref, cwv_ref, cbg_ref, cbv_ref,
                wd_ref, out_ref, xn_ref, *, n_sub, tiles_per_batch, real_only, rc):
    i = pl.program_id(0)
    c = pl.program_id(1)
    tm = h_ref.shape[0]

    @pl.when(c == 0)
    def _():
        _rms_rows(h_ref, gain_ref, xn_ref, HALO, tm, rc)
        _rms_rows(hp_ref, gain_ref, xn_ref, 0, HALO, HALO)
        _rms_rows(hn_ref, gain_ref, xn_ref, HALO + tm, HALO, HALO)

        if not real_only:
            @pl.when(i == 0)
            def _():
                xn_ref[0:HALO, :] = jnp.zeros((HALO, xn_ref.shape[1]), xn_ref.dtype)

        @pl.when(i == pl.num_programs(0) - 1)
        def _():
            xn_ref[HALO + tm:, :] = jnp.zeros((HALO, xn_ref.shape[1]), xn_ref.dtype)

        out_ref[...] = h_ref[...]

    def conv(row0, rows, w_ref, cw_ref, cb_ref):
        u = jnp.dot(xn_ref[row0:row0 + rows + 2 * HALO, :], w_ref[...],
                    preferred_element_type=F32)
        cw = cw_ref[...]
        out = cb_ref[...]
        for tap in range(CONV_W):
            off = HALO - CONV_W // 2 + tap
            out = out + u[off:off + rows] * cw[tap:tap + 1]
        return out

    rows = tm // n_sub
    acts = []
    for s in range(n_sub):
        g = conv(s * rows, rows, wg_ref, cwg_ref, cbg_ref)
        val = conv(s * rows, rows, wv_ref, cwv_ref, cbv_ref)
        acts.append((g * _sigmoid(g) * val).astype(BF16))
    for s in range(n_sub):
        sl = slice(s * rows, (s + 1) * rows)
        out_ref[sl, :] += jnp.dot(acts[s], wd_ref[...], preferred_element_type=F32)

    if not real_only:
        @pl.when(jnp.logical_and(c == pl.num_programs(1) - 1, i % tiles_per_batch == 0))
        def _():
            out_ref[0:PAD, :] = jnp.zeros((PAD, out_ref.shape[1]), F32)


def _ffn(h, gain, w_up, conv_w, conv_b, w_down, tp, tm, tf, real_only):
    n, d = h.shape
    d_ff = w_down.shape[0]
    n_ff = d_ff // tf
    if real_only:
        tiles_per_batch = (tp - BLK) // tm
        n_out = n // tp * (tp - BLK)
        row0 = lambda i: (i // tiles_per_batch) * tp + BLK + (i % tiles_per_batch) * tm
        window = lambda rows, start: pl.BlockSpec((pl.Element(rows), pl.Element(d)),
                                                  lambda i, c: (pl.multiple_of(start(i), HALO), 0))
        h_specs = [window(tm, row0), window(HALO, lambda i: row0(i) - HALO),
                   window(HALO, lambda i: jnp.minimum(row0(i) + tm, n - HALO))]
    else:
        tiles_per_batch = tp // tm
        n_out = n
        hb = tm // HALO
        last_halo = n // HALO - 1
        h_specs = [pl.BlockSpec((tm, d), lambda i, c: (i, 0)),
                   pl.BlockSpec((HALO, d), lambda i, c: (jnp.maximum(i * hb - 1, 0), 0)),
                   pl.BlockSpec((HALO, d), lambda i, c: (jnp.minimum((i + 1) * hb, last_halo), 0))]
    return pl.pallas_call(
        functools.partial(_ffn_kernel, n_sub=FFN_SUBTILES, tiles_per_batch=tiles_per_batch,
                          real_only=real_only, rc=_pick_tile(tm, (128, 64))),
        grid=(n_out // tm, n_ff),
        in_specs=h_specs + [
            pl.BlockSpec((1, d), lambda i, c: (0, 0)),
            pl.BlockSpec((d, tf), lambda i, c: (0, c)),
            pl.BlockSpec((d, tf), lambda i, c: (0, n_ff + c)),
            pl.BlockSpec((CONV_W, tf), lambda i, c: (0, c)),
            pl.BlockSpec((CONV_W, tf), lambda i, c: (0, n_ff + c)),
            pl.BlockSpec((1, tf), lambda i, c: (0, c)),
            pl.BlockSpec((1, tf), lambda i, c: (0, n_ff + c)),
            pl.BlockSpec((tf, d), lambda i, c: (c, 0)),
        ],
        out_specs=pl.BlockSpec((tm, d), lambda i, c: (i, 0)),
        out_shape=jax.ShapeDtypeStruct((n_out, d), F32),
        scratch_shapes=[pltpu.VMEM((tm + 2 * HALO, d), BF16)],
        compiler_params=_params(("parallel", "arbitrary")),
    )(h, h, h, gain, w_up, w_up, conv_w, conv_w, conv_b, conv_b, w_down)


def kernel(x, meta_tokens, norm_mix, norm_ffn, ab_w_in, ab_gate_bias, ab_q_gain, ab_k_gain, ab_rel_bias, ab_ml_gain, ab_w_out, c_w_in, c_gate_up, c_gate_bias, c_head_gain, c_w_out, ffn_w_up, ffn_conv_w, ffn_conv_b, ffn_w_down):
    batch, seq, d = x.shape
    d_ff = ffn_w_down.shape[1]
    assert seq % (2 * NA_GROUP * GRID_W) == 0 and seq // GRID_W >= NA_UNION and d % LANES == 0
    tp = BLK + seq
    nb = tp // BLK
    n = batch * tp
    tm_proj = _pick_tile(tp, (1280, 1024, 768, 512, 256))
    tm_out = _pick_tile(tp, (640, 512, 256))
    tm_ffn = _pick_tile(tp, (832, 640, 512, 256))
    tm_ffn_real = _pick_tile(seq, (1024, 512, 256))
    tn_proj = 1024
    tf = _pick_tile(d_ff, (512, 256, 128))

    meta = jnp.broadcast_to(meta_tokens.astype(x.dtype)[None], (batch, N_META, d))
    h = jnp.concatenate([jnp.zeros((batch, PAD, d), x.dtype), meta, x], axis=1).reshape(n, d)

    n_main0 = 3 * NA_WIDTH + 2 * ML_QK_WIDTH + 2 * ML_WIDTH
    w_in = ab_w_in[0]
    ones = lambda k: jnp.ones((k,), F32)
    colscale0 = jnp.concatenate([
        jnp.tile(ab_q_gain[0].astype(F32), NA_HEADS), jnp.tile(ab_k_gain[0].astype(F32), NA_HEADS),
        ones(NA_WIDTH), ones(ML_QK_WIDTH), jnp.full((ML_QK_WIDTH,), ML_QK ** -0.5, F32),
        ones(2 * ML_WIDTH)])[None]
    proj0, graw = _in_proj(h, norm_mix[0][None].astype(F32), w_in.astype(BF16), colscale0,
                           w_in[:, n_main0:].T.astype(BF16), n_rms=2 * NA_WIDTH // tn_proj, tail_nt=True,
                           tm=tm_proj, tn=tn_proj)
    gr, gc = _mlstm_gates(graw, ab_gate_bias[0].astype(F32)[:, None], tp, tm_proj)
    y_na = _na_attention(proj0, _na_bias_tables(ab_rel_bias[0], seq // GRID_W), batch, tp)
    y_f, y_b = _mlstm(proj0, gr, gc, batch, nb)
    h = _out_proj0(y_na, y_f, y_b, proj0, ab_ml_gain[0][None].astype(F32), ab_w_out[0].astype(BF16),
                   h, tp, tm_out)
    h = _ffn(h, norm_ffn[0][None].astype(F32), ffn_w_up[0].astype(BF16), ffn_conv_w[0].astype(F32),
             ffn_conv_b[0][None].astype(F32), ffn_w_down[0].astype(BF16), tp, tm_ffn, tf, False)

    n_main1 = 2 * GLA_KW + 2 * GLA_VW
    w_in = c_w_in[0]
    colscale1 = jnp.concatenate([jnp.full((GLA_KW,), GLA_DK ** -0.5, F32), ones(n_main1 - GLA_KW)])[None]
    w_tail = jnp.pad(w_in[:, n_main1:], ((0, 0), (0, LANES - 2 * GLA_RANK))).astype(BF16)
    proj1, lr = _in_proj(h, norm_mix[1][None].astype(F32), w_in.astype(BF16), colscale1, w_tail,
                         n_rms=0, tail_nt=False, tm=tm_proj, tn=tn_proj)
    gu = c_gate_up[0].astype(BF16).reshape(2, GLA_RANK, GLA_HEADS, GLA_DK).transpose(0, 2, 1, 3)
    gate_up = jnp.stack([jnp.pad(gu[0], ((0, 0), (0, LANES - GLA_RANK), (0, 0))),
                         jnp.pad(gu[1], ((0, 0), (GLA_RANK, LANES - 2 * GLA_RANK), (0, 0)))])
    gate_bias = c_gate_bias[0].astype(F32).reshape(2, GLA_HEADS, 1, GLA_DK)
    o_f, o_b = _gla(proj1, lr, gate_up, gate_bias, batch, nb)
    h = _out_proj1(o_f, o_b, proj1, c_head_gain[0][None].astype(F32), c_w_out[0].astype(BF16), h, tp, tm_out)
    out = _ffn(h, norm_ffn[1][None].astype(F32), ffn_w_up[1].astype(BF16), ffn_conv_w[1].astype(F32),
               ffn_conv_b[1][None].astype(F32), ffn_w_down[1].astype(BF16), tp, tm_ffn_real, tf, True)
    return out.reshape(batch, seq, d)
```

```python
import functools

import numpy as np
import jax
import jax.numpy as jnp
from jax import lax
from jax.experimental import pallas as pl
from jax.experimental.pallas import tpu as pltpu

F32 = jnp.float32
BF16 = jnp.bfloat16
EPS = 1e-6

N_META = 16
GRID_W = 64
NA_HEADS = 8
NA_DIM = 128
NA_WIDTH = NA_HEADS * NA_DIM
WIN_H = 8
WIN_W = 16
ML_HEADS = 4
ML_QK = 128
ML_V = 256
ML_QK_WIDTH = ML_HEADS * ML_QK
ML_WIDTH = ML_HEADS * ML_V
N_ML_GATES = 4 * ML_HEADS
GLA_HEADS = 4
GLA_DK = 256
GLA_DV = 512
GLA_KW = GLA_HEADS * GLA_DK
GLA_VW = GLA_HEADS * GLA_DV
GLA_RANK = 16
GLA_TAU = 16.0
GLA_CHUNK = 64
CONV_W = 3

LANES = 128
BLK = 256
PAD = BLK - N_META
NA_GROUP = 4
NA_UNION = NA_GROUP + WIN_H - 1
NA_UNROLL = 4
HALO = 16
FFN_SUBTILES = 2
GLA_GROUP = 4
VMEM_LIMIT = 56 * 1024 * 1024

_NT = (((1,), (1,)), ((), ()))
_TN = (((0,), (0,)), ((), ()))


def _log_sigmoid(x):
    return jnp.minimum(x, 0.0) - jnp.log(1.0 + jnp.exp(-jnp.abs(x)))


def _sigmoid(x):
    return 1.0 / (1.0 + jnp.exp(-x))


def _pick_tile(n, prefs):
    for t in prefs:
        if n % t == 0:
            return t
    raise ValueError(f"no tile in {prefs} divides {n}")


def _params(sem, vmem=VMEM_LIMIT):
    return pltpu.CompilerParams(dimension_semantics=sem, vmem_limit_bytes=vmem)


def _rms_rows(x_ref, gain_ref, dst_ref, dst_row0, n_rows, rc):
    g = gain_ref[...]

    def body(r, carry):
        r0 = pl.multiple_of(r * rc, rc)
        xx = x_ref[pl.ds(r0, rc), :]
        ms = jnp.mean(xx * xx, axis=-1, keepdims=True)
        dst_ref[pl.ds(dst_row0 + r0, rc), :] = (xx * lax.rsqrt(ms + EPS) * g).astype(dst_ref.dtype)
        return carry

    lax.fori_loop(0, n_rows // rc, body, 0)


def _row_valid(tile_idx, tiles_per_batch, tm):
    it = tile_idx % tiles_per_batch
    row = lax.broadcasted_iota(jnp.int32, (tm, 1), 0)
    return (row + it * tm) >= PAD


def _in_proj_kernel(x_ref, head_ref, gain_ref, w_ref, cs_ref, wt_ref, out_ref, tail_ref, xn_ref, *,
                    n_rms, tail_nt, tiles_per_batch, rc):
    j = pl.program_id(1)
    tm = x_ref.shape[0]
    tn = w_ref.shape[1]

    @pl.when(j == 0)
    def _():
        if tiles_per_batch is None:
            _rms_rows(x_ref, gain_ref, xn_ref, 0, tm, rc)
        else:
            first = pl.program_id(0) % tiles_per_batch == 0

            @pl.when(first)
            def _():
                _rms_rows(head_ref, gain_ref, xn_ref, 0, BLK, rc)
                _rms_rows(x_ref, gain_ref, xn_ref, BLK, tm - BLK, rc)

            @pl.when(jnp.logical_not(first))
            def _():
                _rms_rows(x_ref, gain_ref, xn_ref, 0, tm, rc)

        if tail_nt:
            tail_ref[...] = lax.dot_general(wt_ref[...], xn_ref[...], _NT, preferred_element_type=F32)
        else:
            tail_ref[...] = jnp.dot(xn_ref[...], wt_ref[...],
                                    preferred_element_type=F32).astype(tail_ref.dtype)

    @pl.when(j > 0)
    def _():
        acc = jnp.dot(xn_ref[...], w_ref[...].astype(BF16), preferred_element_type=F32)
        cs = cs_ref[...]
        if n_rms > 0:
            @pl.when(j <= n_rms)
            def _():
                for hh in range(tn // LANES):
                    sl = slice(hh * LANES, (hh + 1) * LANES)
                    a = acc[:, sl]
                    ms = jnp.mean(a * a, axis=-1, keepdims=True)
                    out_ref[:, sl] = (a * lax.rsqrt(ms + EPS) * cs[:, sl]).astype(out_ref.dtype)

            @pl.when(j > n_rms)
            def _():
                out_ref[...] = (acc * cs).astype(out_ref.dtype)
        else:
            out_ref[...] = (acc * cs).astype(out_ref.dtype)


def _stream_window_spec(tp, tm, d):
    tiles_per_batch = tp // tm

    def index_map(i, *_):
        start = jnp.maximum((i % tiles_per_batch) * tm - BLK, 0)
        return i // tiles_per_batch, pl.multiple_of(start, BLK), 0

    return pl.BlockSpec((None, pl.Element(tm), pl.Element(d)), index_map)


def _in_proj(h, head, gain, w, colscale, w_tail, *, n_rms, tail_nt, tm, tn, tp=None):
    if tp is None:
        n, d = h.shape
        x_spec = pl.BlockSpec((tm, d), lambda i, j: (i, 0))
    else:
        d = h.shape[2]
        n = h.shape[0] * tp
        x_spec = _stream_window_spec(tp, tm, d)
    width_main = colscale.shape[1]
    n_main = width_main // tn
    if tail_nt:
        n_tail = w_tail.shape[0]
        wt_spec = pl.BlockSpec((n_tail, d), lambda i, j: (0, 0))
        tail_shape = jax.ShapeDtypeStruct((n_tail, n), F32)
        tail_spec = pl.BlockSpec((n_tail, tm), lambda i, j: (0, i))
    else:
        n_tail = w_tail.shape[1]
        wt_spec = pl.BlockSpec((d, n_tail), lambda i, j: (0, 0))
        tail_shape = jax.ShapeDtypeStruct((n, n_tail), BF16)
        tail_spec = pl.BlockSpec((tm, n_tail), lambda i, j: (i, 0))
    col = lambda i, j: (0, jnp.maximum(j - 1, 0))
    kern = functools.partial(_in_proj_kernel, n_rms=n_rms, tail_nt=tail_nt,
                             tiles_per_batch=None if tp is None else tp // tm, rc=128)
    return pl.pallas_call(
        kern,
        grid=(n // tm, n_main + 1),
        in_specs=[
            x_spec,
            pl.BlockSpec((BLK, d), lambda i, j: (0, 0)),
            pl.BlockSpec((1, d), lambda i, j: (0, 0)),
            pl.BlockSpec((d, tn), col),
            pl.BlockSpec((1, tn), col),
            wt_spec,
        ],
        out_specs=[
            pl.BlockSpec((tm, tn), lambda i, j: (i, jnp.maximum(j - 1, 0))),
            tail_spec,
        ],
        out_shape=[jax.ShapeDtypeStruct((n, width_main), BF16), tail_shape],
        scratch_shapes=[pltpu.VMEM((tm, d), BF16)],
        compiler_params=_params(("parallel", "arbitrary")),
    )(h, head, gain, w, colscale, w_tail)


def _mlstm_gates_kernel(graw_ref, bias_ref, gr_ref, gc_ref, *, tiles_per_batch):
    width = graw_ref.shape[1]
    g = graw_ref[...] + bias_ref[...]
    row = lax.broadcasted_iota(jnp.int32, g.shape, 0)
    pos = lax.broadcasted_iota(jnp.int32, g.shape, 1)
    lane = pos & (BLK - 1)
    is_forget = (row & 4) != 0
    valid = (pos + (pl.program_id(0) % tiles_per_batch) * width) >= PAD
    lf = jnp.where(valid, _log_sigmoid(g), 0.0)
    pre = lf
    suf = lf
    d = 1
    while d < BLK:
        pre = pre + jnp.where(lane >= d, pltpu.roll(pre, d, axis=1), 0.0)
        suf = suf + jnp.where(lane < BLK - d, pltpu.roll(suf, width - d, axis=1), 0.0)
        d *= 2
    ig = jnp.where(valid, g, -jnp.inf)
    out = jnp.where(is_forget, jnp.where(row >= 8, suf, pre), ig)
    a = out - pltpu.roll(out, N_ML_GATES - 4, axis=0)
    pmax = a
    smax = a
    d = 1
    while d < BLK:
        pmax = jnp.maximum(pmax, jnp.where(lane >= d, pltpu.roll(pmax, d, axis=1), -jnp.inf))
        smax = jnp.maximum(smax, jnp.where(lane < BLK - d, pltpu.roll(smax, width - d, axis=1), -jnp.inf))
        d *= 2
    runmax = jnp.where(is_forget, 0.0, jnp.where(row >= 8, smax, pmax))
    both = jnp.concatenate([out, runmax], axis=0)
    gr_ref[...] = both
    full = jnp.concatenate([both, jnp.zeros((LANES - 2 * N_ML_GATES, width), F32)], axis=0)
    gc_ref[...] = full.T


def _mlstm_gates(graw, bias, tp, width):
    n = graw.shape[1]
    return pl.pallas_call(
        functools.partial(_mlstm_gates_kernel, tiles_per_batch=tp // width),
        grid=(n // width,),
        in_specs=[pl.BlockSpec((N_ML_GATES, width), lambda i: (0, i)),
                  pl.BlockSpec((N_ML_GATES, 1), lambda i: (0, 0))],
        out_specs=[pl.BlockSpec((2 * N_ML_GATES, width), lambda i: (0, i)),
                   pl.BlockSpec((width, LANES), lambda i: (i, 0))],
        out_shape=[jax.ShapeDtypeStruct((2 * N_ML_GATES, n), F32),
                   jax.ShapeDtypeStruct((n, LANES), F32)],
        compiler_params=_params(("parallel",)),
    )(graw, bias)


def _na_bias_tables(rel_bias, rows):
    a = np.arange(NA_GROUP)[:, None]
    u = np.arange(NA_UNION)[None, :]
    half = WIN_H // 2
    vis_first = (u < WIN_H) & (a >= 0)
    dr_first = u - a + WIN_H - 1
    vis_mid = (u >= a) & (u < a + WIN_H)
    dr_mid = u - a + WIN_H - 1 - half
    lo = NA_UNION - WIN_H
    vis_last = (u >= lo) & (a >= 0)
    dr_last = u - a + (WIN_H - 1) - (NA_UNION - NA_GROUP)
    vis_r = np.stack([vis_first, vis_mid, vis_last])
    dr = np.clip(np.stack([dr_first, dr_mid, dr_last]), 0, 2 * WIN_H - 2)
    qc = np.arange(GRID_W)[:, None]
    kc = np.arange(GRID_W)[None, :]
    c0 = np.clip(qc - WIN_W // 2, 0, GRID_W - WIN_W)
    vis_c = (kc >= c0) & (kc < c0 + WIN_W)
    dc = np.clip(kc - qc + WIN_W - 1, 0, 2 * WIN_W - 2)
    sel_c = (dc[..., None] == np.arange(2 * WIN_W - 1)).astype(np.float32)
    cols = jnp.einsum('hrd,qkd->hrqk', rel_bias.astype(F32), sel_c, precision=lax.Precision.HIGHEST)
    cols = jnp.where(vis_c, cols, -jnp.inf)
    hidden = jnp.full((NA_HEADS, GRID_W, GRID_W), -jnp.inf, F32)
    return jnp.stack([
        jnp.concatenate([
            jnp.concatenate([cols[:, dr[c, qa, ku]] if vis_r[c, qa, ku] else hidden
                             for ku in range(NA_UNION)], axis=2)
            for qa in range(NA_GROUP)], axis=1)
        for c in range(3)])


def _na_kernel(q_ref, k_ref, v_ref, bias_ref, out_ref, *, rows):
    n_groups = rows // NA_GROUP
    gq = NA_GROUP * GRID_W
    uk = NA_UNION * GRID_W
    scale = NA_DIM ** -0.5
    km = k_ref[PAD:BLK, :]
    vm = v_ref[PAD:BLK, :]

    def attend(q, kw, vw, bias):
        s_m = lax.dot_general(q, km, _NT, preferred_element_type=F32) * scale
        m = jnp.max(s_m, axis=-1, keepdims=True)
        if kw is not None:
            s = lax.dot_general(q, kw, _NT, preferred_element_type=F32) * scale + bias
            m = jnp.maximum(m, jnp.max(s, axis=-1, keepdims=True))
        p_m = jnp.exp(s_m - m)
        den = jnp.sum(p_m, axis=-1, keepdims=True)
        y = jnp.dot(p_m.astype(BF16), vm, preferred_element_type=F32)
        if kw is not None:
            p = jnp.exp(s - m)
            den = den + jnp.sum(p, axis=-1, keepdims=True)
            y = y + jnp.dot(p.astype(BF16), vw, preferred_element_type=F32)
        return y / den

    out_ref[0:PAD, :] = jnp.zeros((PAD, NA_DIM), out_ref.dtype)
    out_ref[PAD:BLK, :] = attend(q_ref[PAD:BLK, :], None, None, None).astype(out_ref.dtype)

    def group(g):
        q0 = pl.multiple_of(BLK + g * gq, gq)
        u0 = jnp.clip(g * NA_GROUP - WIN_H // 2, 0, rows - NA_UNION)
        k0 = pl.multiple_of(BLK + u0 * GRID_W, GRID_W)
        cls = jnp.where(g == 0, 0, jnp.where(g == n_groups - 1, 2, 1))
        return attend(q_ref[pl.ds(q0, gq), :], k_ref[pl.ds(k0, uk), :], v_ref[pl.ds(k0, uk), :],
                      bias_ref[cls])

    def body(gg, carry):
        g = gg * NA_UNROLL
        y = jnp.concatenate([group(g + u) for u in range(NA_UNROLL)], axis=0)
        out_ref[pl.ds(pl.multiple_of(BLK + g * gq, gq), NA_UNROLL * gq), :] = y.astype(out_ref.dtype)
        return carry

    lax.fori_loop(0, n_groups // NA_UNROLL, body, 0)


def _na_attention(proj, bias, batch, tp):
    rows = (tp - BLK) // GRID_W
    n = proj.shape[0]
    gq = NA_GROUP * GRID_W
    uk = NA_UNION * GRID_W
    return pl.pallas_call(
        functools.partial(_na_kernel, rows=rows),
        grid=(batch, NA_HEADS),
        in_specs=[
            pl.BlockSpec((tp, NA_DIM), lambda b, h: (b, h)),
            pl.BlockSpec((tp, NA_DIM), lambda b, h: (b, NA_HEADS + h)),
            pl.BlockSpec((tp, NA_DIM), lambda b, h: (b, 2 * NA_HEADS + h)),
            pl.BlockSpec((3, None, gq, uk), lambda b, h: (0, h, 0, 0)),
        ],
        out_specs=pl.BlockSpec((tp, NA_DIM), lambda b, h: (b, h)),
        out_shape=jax.ShapeDtypeStruct((n, NA_WIDTH), BF16),
        compiler_params=_params(("parallel", "parallel")),
    )(proj, proj, proj, bias)


def _mlstm_dir(q_ref, k_ref, v_ref, gr_ref, gc_ref, y_ref, c_ref, m_ref, head, rev):
    q = q_ref[:, head * ML_QK:(head + 1) * ML_QK]
    k = k_ref[:, head * ML_QK:(head + 1) * ML_QK]
    v = v_ref[:, head * ML_V:(head + 1) * ML_V]
    base = 8 if rev else 0
    a_r = gr_ref[base + head:base + head + 1, :] - gr_ref[base + 4 + head:base + 5 + head, :]
    a_c = gc_ref[:, base + head:base + head + 1] - gc_ref[:, base + 4 + head:base + 5 + head]
    b_c = gc_ref[:, base + 4 + head:base + 5 + head]
    run_c = gc_ref[:, N_ML_GATES + base + head:N_ML_GATES + base + head + 1]
    m = m_ref[head]
    m_rel = jnp.broadcast_to(jnp.maximum(m, run_c), (BLK, LANES))
    ti = lax.broadcasted_iota(jnp.int32, (BLK, BLK), 0)
    si = lax.broadcasted_iota(jnp.int32, (BLK, BLK), 1)
    mask = (si >= ti) if rev else (si <= ti)
    arg = jnp.where(mask, a_r - jnp.concatenate([m_rel] * (BLK // LANES), axis=1), -jnp.inf)
    p = jnp.exp(arg) * lax.dot_general(q, k, _NT, preferred_element_type=F32)
    w_prev = jnp.exp(m - m_rel)
    v_ext = jnp.concatenate([v, jnp.ones((BLK, LANES), BF16)], axis=1)
    c_ext = c_ref[head]
    num = (jnp.concatenate([w_prev] * (c_ext.shape[1] // LANES), axis=1)
           * jnp.dot(q, c_ext.astype(BF16), preferred_element_type=F32)
           + jnp.dot(p.astype(BF16), v_ext, preferred_element_type=F32))
    den = num[:, ML_V:]
    inv = 1.0 / jnp.maximum(jnp.abs(den), jnp.exp(-(jnp.broadcast_to(b_c, (BLK, LANES)) + m_rel)))
    y_ref[:, head * ML_V:(head + 1) * ML_V] = (
        num[:, :ML_V] * jnp.concatenate([inv] * (ML_V // LANES), axis=1)).astype(y_ref.dtype)
    last = 0 if rev else BLK - 1
    b_end = gr_ref[base + 4 + head:base + 5 + head, last:last + 1]
    m_end = jnp.maximum(m, gr_ref[N_ML_GATES + base + head:N_ML_GATES + base + head + 1, last:last + 1])
    w_tok = jnp.exp(a_c - m_end)
    c_ref[head] = jnp.exp(m - m_end) * c_ext + lax.dot_general(
        k, (w_tok * v_ext.astype(F32)).astype(BF16), _TN, preferred_element_type=F32)
    m_ref[head] = b_end + m_end


def _mlstm_kernel(qf, kf, vf, grf, gcf, qb, kb, vb, grb, gcb, yf, yb, cf, mf, cb, mb):
    @pl.when(pl.program_id(1) == 0)
    def _():
        cf[...] = jnp.zeros(cf.shape, F32)
        cb[...] = jnp.zeros(cb.shape, F32)
        mf[...] = jnp.zeros(mf.shape, F32)
        mb[...] = jnp.zeros(mb.shape, F32)

    for head in range(ML_HEADS):
        _mlstm_dir(qf, kf, vf, grf, gcf, yf, cf, mf, head, False)
        _mlstm_dir(qb, kb, vb, grb, gcb, yb, cb, mb, head, True)


def _mlstm(proj, gr, gc, batch, nb):
    n = proj.shape[0]
    q0 = 3 * NA_WIDTH // ML_QK_WIDTH
    k0 = q0 + 1
    v0 = (3 * NA_WIDTH + 2 * ML_QK_WIDTH) // ML_WIDTH
    fwd = lambda b, s: b * nb + s
    bwd = lambda b, s: b * nb + nb - 1 - s

    def specs(pos):
        return [
            pl.BlockSpec((BLK, ML_QK_WIDTH), lambda b, s: (pos(b, s), q0)),
            pl.BlockSpec((BLK, ML_QK_WIDTH), lambda b, s: (pos(b, s), k0)),
            pl.BlockSpec((BLK, ML_WIDTH), lambda b, s: (pos(b, s), v0)),
            pl.BlockSpec((2 * N_ML_GATES, BLK), lambda b, s: (0, pos(b, s))),
            pl.BlockSpec((BLK, LANES), lambda b, s: (pos(b, s), 0)),
        ]

    state = [pltpu.VMEM((ML_HEADS, ML_QK, ML_V + LANES), F32), pltpu.VMEM((ML_HEADS, 1, 1), F32)]
    return pl.pallas_call(
        _mlstm_kernel,
        grid=(batch, nb),
        in_specs=specs(fwd) + specs(bwd),
        out_specs=[pl.BlockSpec((BLK, ML_WIDTH), lambda b, s: (fwd(b, s), 0)),
                   pl.BlockSpec((BLK, ML_WIDTH), lambda b, s: (bwd(b, s), 0))],
        out_shape=[jax.ShapeDtypeStruct((n, ML_WIDTH), BF16)] * 2,
        scratch_shapes=state + state,
        compiler_params=_params(("parallel", "arbitrary")),
    )(proj, proj, proj, gr, gc, proj, proj, proj, gr, gc)


def _gla_dir(q_ref, k_ref, v_ref, lr_ref, gu_ref, gb_ref, tri_ref, o_ref, st_ref, valid, head, rev):
    dk = GLA_DK
    ksl = slice(head * GLA_DK, (head + 1) * GLA_DK)
    vsl = slice(head * GLA_DV, (head + 1) * GLA_DV)
    z = jnp.dot(lr_ref[...], gu_ref[head], preferred_element_type=F32) + gb_ref[head]
    la = jnp.where(valid, _log_sigmoid(z) * (1.0 / GLA_TAU), 0.0)
    t1 = la.astype(BF16)
    t2 = (la - t1.astype(F32)).astype(BF16)
    tri = tri_ref[...]
    b = jnp.dot(tri, t1, preferred_element_type=F32) + jnp.dot(tri, t2, preferred_element_type=F32)
    q = q_ref[:, ksl].astype(F32)
    k = k_ref[:, ksl].astype(F32)
    b_end = b[0:1] if rev else b[BLK - 1:BLK]
    q_in = (q * jnp.exp(b)).astype(BF16)
    k_dec = (k * jnp.exp(b_end - b)).astype(BF16)
    sub = GLA_CHUNK
    ti = lax.broadcasted_iota(jnp.int32, (sub, BLK), 0)
    si = lax.broadcasted_iota(jnp.int32, (sub, BLK), 1)

    def at_rows(x, row0):
        parts = [jnp.zeros((n, dk), BF16) for n in (row0,) if n] + [x]
        after = BLK - row0 - x.shape[0]
        if after:
            parts.append(jnp.zeros((after, dk), BF16))
        return jnp.concatenate(parts, axis=0) if len(parts) > 1 else x

    a_rows = []
    for i in range(BLK // sub):
        lo, hi = i * sub, (i + 1) * sub
        b_i, q_i = b[lo:hi], q[lo:hi]
        mu = b[lo + sub // 2:lo + sub // 2 + 1]
        lhs = (q_i * jnp.exp(b_i - mu)).astype(BF16)
        rhs = at_rows((k[lo:hi] * jnp.exp(mu - b_i)).astype(BF16), lo)
        n_prev = BLK - hi if rev else lo
        if n_prev:
            prev = slice(hi, BLK) if rev else slice(0, lo)
            rho = b[hi:hi + 1] if rev else b[lo - 1:lo]
            k_prev = at_rows((k[prev] * jnp.exp(rho - b[prev])).astype(BF16), hi if rev else 0)
            lhs = jnp.concatenate([(q_i * jnp.exp(b_i - rho)).astype(BF16), lhs], axis=1)
            rhs = jnp.concatenate([k_prev, rhs], axis=1)
        a_i = lax.dot_general(lhs, rhs, _NT, preferred_element_type=F32)
        seen = (si >= ti + lo) if rev else (si <= ti + lo)
        a_rows.append(jnp.where(seen, a_i, 0.0).astype(BF16))
    a = jnp.concatenate(a_rows, axis=0)
    st = st_ref[head]
    v = v_ref[:, vsl]
    o_ref[:, vsl] = (lax.dot_general(q_in, st.astype(BF16), _NT, preferred_element_type=F32)
                     + jnp.dot(a, v, preferred_element_type=F32)).astype(o_ref.dtype)
    st_ref[head] = st * jnp.exp(b_end) + lax.dot_general(v, k_dec, _TN, preferred_element_type=F32)


def _gla_kernel(qf, kf, vf, lrf, guf, gbf, bdf, qb, kb, vb, lrb, gub, gbb, bdb, of, ob, sf, sb, *, nb):
    s = pl.program_id(2)

    @pl.when(s == 0)
    def _():
        sf[...] = jnp.zeros(sf.shape, F32)
        sb[...] = jnp.zeros(sb.shape, F32)

    row = lax.broadcasted_iota(jnp.int32, (BLK, 1), 0)
    for head in range(GLA_GROUP):
        _gla_dir(qf, kf, vf, lrf, guf, gbf, bdf, of, sf, (row + s * BLK) >= PAD, head, False)
        _gla_dir(qb, kb, vb, lrb, gub, gbb, bdb, ob, sb, (row + (nb - 1 - s) * BLK) >= PAD, head, True)


def _gla(proj, lr, gate_up, gate_bias, batch, nb):
    n = proj.shape[0]
    grp = GLA_GROUP
    k0 = GLA_KW // (grp * GLA_DK)
    v0 = 2 * GLA_KW // (grp * GLA_DV)
    fwd = lambda b, h, s: b * nb + s
    bwd = lambda b, h, s: b * nb + nb - 1 - s
    t = np.arange(BLK)
    bd_f = jnp.asarray(t[None, :] <= t[:, None], BF16)
    bd_b = jnp.asarray(t[None, :] >= t[:, None], BF16)

    def specs(pos, direction):
        return [
            pl.BlockSpec((BLK, grp * GLA_DK), lambda b, h, s: (pos(b, h, s), h)),
            pl.BlockSpec((BLK, grp * GLA_DK), lambda b, h, s: (pos(b, h, s), k0 + h)),
            pl.BlockSpec((BLK, grp * GLA_DV), lambda b, h, s: (pos(b, h, s), v0 + h)),
            pl.BlockSpec((BLK, LANES), lambda b, h, s: (pos(b, h, s), 0)),
            pl.BlockSpec((None, grp, LANES, GLA_DK), lambda b, h, s: (direction, h, 0, 0)),
            pl.BlockSpec((None, grp, 1, GLA_DK), lambda b, h, s: (direction, h, 0, 0)),
            pl.BlockSpec((BLK, BLK), lambda b, h, s: (0, 0)),
        ]

    state = pltpu.VMEM((grp, GLA_DV, GLA_DK), F32)
    return pl.pallas_call(
        functools.partial(_gla_kernel, nb=nb),
        grid=(batch, GLA_HEADS // grp, nb),
        in_specs=specs(fwd, 0) + specs(bwd, 1),
        out_specs=[pl.BlockSpec((BLK, grp * GLA_DV), lambda b, h, s: (fwd(b, h, s), h)),
                   pl.BlockSpec((BLK, grp * GLA_DV), lambda b, h, s: (bwd(b, h, s), h))],
        out_shape=[jax.ShapeDtypeStruct((n, GLA_VW), BF16)] * 2,
        scratch_shapes=[state, state],
        compiler_params=_params(("parallel", "parallel", "arbitrary")),
    )(proj, proj, proj, lr, gate_up, gate_bias, bd_f, proj, proj, proj, lr, gate_up, gate_bias, bd_b)


def _gated_head_rms(yf_ref, yb_ref, gate_ref, gain_ref, n_heads, act):
    y = yf_ref[...].astype(F32) + yb_ref[...].astype(F32)
    gate = act(gate_ref[...].astype(F32))
    gain = gain_ref[...]
    hd = y.shape[1] // n_heads
    outs = []
    for hh in range(n_heads):
        sl = slice(hh * hd, (hh + 1) * hd)
        a = y[:, sl]
        ms = jnp.mean(a * a, axis=-1, keepdims=True)
        outs.append((a * lax.rsqrt(ms + EPS) * gain[:, sl] * gate[:, sl]).astype(BF16))
    return jnp.concatenate(outs, axis=1)


def _out_proj0_kernel(ya_ref, yf_ref, yb_ref, gate_ref, gain_ref, wa_ref, wb_ref, x_ref, head_ref,
                      out_ref, *, tiles_per_batch):
    tm = out_ref.shape[0]
    y_ml = _gated_head_rms(yf_ref, yb_ref, gate_ref, gain_ref, ML_HEADS, _sigmoid)
    mix = (jnp.dot(ya_ref[...], wa_ref[...], preferred_element_type=F32)
           + jnp.dot(y_ml, wb_ref[...], preferred_element_type=F32))
    first = pl.program_id(0) % tiles_per_batch == 0

    @pl.when(first)
    def _():
        row = lax.broadcasted_iota(jnp.int32, (BLK, 1), 0)
        out_ref[0:BLK, :] = jnp.where(row >= PAD, head_ref[...] + mix[0:BLK], 0.0)
        if tm > BLK:
            out_ref[BLK:, :] = x_ref[0:tm - BLK, :] + mix[BLK:]

    @pl.when(jnp.logical_not(first))
    def _():
        out_ref[...] = x_ref[...] + mix


def _out_proj1_kernel(yf_ref, yb_ref, gate_ref, gain_ref, w_ref, h_ref, out_ref, *, tiles_per_batch):
    tm = h_ref.shape[0]
    y = _gated_head_rms(yf_ref, yb_ref, gate_ref, gain_ref, GLA_HEADS, lambda r: r * _sigmoid(r))
    out = h_ref[...] + jnp.dot(y, w_ref[...], preferred_element_type=F32)
    out_ref[...] = jnp.where(_row_valid(pl.program_id(0), tiles_per_batch, tm), out, 0.0)


def _out_proj0(y_na, y_f, y_b, proj, gain, w, x, head, tp, tm):
    d = x.shape[2]
    n = x.shape[0] * tp
    assert NA_WIDTH == ML_WIDTH
    gate_blk = (3 * NA_WIDTH + 2 * ML_QK_WIDTH + ML_WIDTH) // ML_WIDTH
    row = lambda i: (i, 0)
    const = lambda i: (0, 0)
    return pl.pallas_call(
        functools.partial(_out_proj0_kernel, tiles_per_batch=tp // tm),
        grid=(n // tm,),
        in_specs=[
            pl.BlockSpec((tm, NA_WIDTH), row),
            pl.BlockSpec((tm, ML_WIDTH), row),
            pl.BlockSpec((tm, ML_WIDTH), row),
            pl.BlockSpec((tm, ML_WIDTH), lambda i: (i, gate_blk)),
            pl.BlockSpec((1, ML_WIDTH), const),
            pl.BlockSpec((NA_WIDTH, d), const, pipeline_mode=pl.Buffered(1)),
            pl.BlockSpec((ML_WIDTH, d), lambda i: (1, 0), pipeline_mode=pl.Buffered(1)),
            _stream_window_spec(tp, tm, d),
            pl.BlockSpec((BLK, d), const),
        ],
        out_specs=pl.BlockSpec((tm, d), row),
        out_shape=jax.ShapeDtypeStruct((n, d), F32),
        compiler_params=_params(("parallel",)),
    )(y_na, y_f, y_b, proj, gain, w, w, x, head)


def _out_proj1(o_f, o_b, proj, gain, w, h, tp, tm):
    n, d = h.shape
    gate_blk = (2 * GLA_KW + GLA_VW) // GLA_VW
    row = lambda i: (i, 0)
    const = lambda i: (0, 0)
    return pl.pallas_call(
        functools.partial(_out_proj1_kernel, tiles_per_batch=tp // tm),
        grid=(n // tm,),
        in_specs=[
            pl.BlockSpec((tm, GLA_VW), row),
            pl.BlockSpec((tm, GLA_VW), row),
            pl.BlockSpec((tm, GLA_VW), lambda i: (i, gate_blk)),
            pl.BlockSpec((1, GLA_VW), const),
            pl.BlockSpec((GLA_VW, d), const, pipeline_mode=pl.Buffered(1)),
            pl.BlockSpec((tm, d), row),
        ],
        out_specs=pl.BlockSpec((tm, d), row),
        out_shape=jax.ShapeDtypeStruct((n, d), F32),
        compiler_params=_params(("parallel",)),
    )(o_f, o_b, proj, gain, w, h)


def _ffn_kernel(h_ref, hp_ref, hn_ref, gain_ref, wg_ref, wv_ref, cwg_ref, cwv_ref, cbg_ref, cbv_ref,
                wd_ref, out_ref, xn_ref, *, n_sub, tiles_per_batch, real_only, rc):
    i = pl.program_id(0)
    c = pl.program_id(1)
    tm = h_ref.shape[0]

    @pl.when(c == 0)
    def _():
        _rms_rows(h_ref, gain_ref, xn_ref, HALO, tm, rc)
        _rms_rows(hp_ref, gain_ref, xn_ref, 0, HALO, HALO)
        _rms_rows(hn_ref, gain_ref, xn_ref, HALO + tm, HALO, HALO)

        if not real_only:
            @pl.when(i == 0)
            def _():
                xn_ref[0:HALO, :] = jnp.zeros((HALO, xn_ref.shape[1]), xn_ref.dtype)

        @pl.when(i == pl.num_programs(0) - 1)
        def _():
            xn_ref[HALO + tm:, :] = jnp.zeros((HALO, xn_ref.shape[1]), xn_ref.dtype)

        out_ref[...] = h_ref[...]

    def conv(row0, rows, w_ref, cw_ref, cb_ref):
        u = jnp.dot(xn_ref[row0:row0 + rows + 2 * HALO, :], w_ref[...],
                    preferred_element_type=F32)
        cw = cw_ref[...]
        out = cb_ref[...]
        for tap in range(CONV_W):
            off = HALO - CONV_W // 2 + tap
            out = out + u[off:off + rows] * cw[tap:tap + 1]
        return out

    rows = tm // n_sub
    acts = []
    for s in range(n_sub):
        g = conv(s * rows, rows, wg_ref, cwg_ref, cbg_ref)
        val = conv(s * rows, rows, wv_ref, cwv_ref, cbv_ref)
        acts.append((g * _sigmoid(g) * val).astype(BF16))
    for s in range(n_sub):
        sl = slice(s * rows, (s + 1) * rows)
        out_ref[sl, :] += jnp.dot(acts[s], wd_ref[...], preferred_element_type=F32)

    if not real_only:
        @pl.when(jnp.logical_and(c == pl.num_programs(1) - 1, i % tiles_per_batch == 0))
        def _():
            out_ref[0:PAD, :] = jnp.zeros((PAD, out_ref.shape[1]), F32)


def _ffn(h, layer, gain, w_up, conv_w, conv_b, w_down, tp, tm, tf, real_only):
    n, d = h.shape
    d_ff = w_down.shape[1]
    n_ff = d_ff // tf
    if real_only:
        tiles_per_batch = (tp - BLK) // tm
        n_out = n // tp * (tp - BLK)
        row0 = lambda i: (i // tiles_per_batch) * tp + BLK + (i % tiles_per_batch) * tm
        window = lambda rows, start: pl.BlockSpec((pl.Element(rows), pl.Element(d)),
                                                  lambda i, c: (pl.multiple_of(start(i), HALO), 0))
        h_specs = [window(tm, row0), window(HALO, lambda i: row0(i) - HALO),
                   window(HALO, lambda i: jnp.minimum(row0(i) + tm, n - HALO))]
    else:
        tiles_per_batch = tp // tm
        n_out = n
        hb = tm // HALO
        last_halo = n // HALO - 1
        h_specs = [pl.BlockSpec((tm, d), lambda i, c: (i, 0)),
                   pl.BlockSpec((HALO, d), lambda i, c: (jnp.maximum(i * hb - 1, 0), 0)),
                   pl.BlockSpec((HALO, d), lambda i, c: (jnp.minimum((i + 1) * hb, last_halo), 0))]
    return pl.pallas_call(
        functools.partial(_ffn_kernel, n_sub=FFN_SUBTILES, tiles_per_batch=tiles_per_batch,
                          real_only=real_only, rc=_pick_tile(tm, (128, 64))),
        grid=(n_out // tm, n_ff),
        in_specs=h_specs + [
            pl.BlockSpec((None, 1, d), lambda i, c: (layer, 0, 0)),
            pl.BlockSpec((None, d, tf), lambda i, c: (layer, 0, c)),
            pl.BlockSpec((None, d, tf), lambda i, c: (layer, 0, n_ff + c)),
            pl.BlockSpec((None, CONV_W, tf), lambda i, c: (layer, 0, c)),
            pl.BlockSpec((None, CONV_W, tf), lambda i, c: (layer, 0, n_ff + c)),
            pl.BlockSpec((None, 1, tf), lambda i, c: (layer, 0, c)),
            pl.BlockSpec((None, 1, tf), lambda i, c: (layer, 0, n_ff + c)),
            pl.BlockSpec((None, tf, d), lambda i, c: (layer, c, 0)),
        ],
        out_specs=pl.BlockSpec((tm, d), lambda i, c: (i, 0)),
        out_shape=jax.ShapeDtypeStruct((n_out, d), F32),
        scratch_shapes=[pltpu.VMEM((tm + 2 * HALO, d), BF16)],
        compiler_params=_params(("parallel", "arbitrary")),
    )(h, h, h, gain, w_up, w_up, conv_w, conv_w, conv_b, conv_b, w_down)


def kernel(x, meta_tokens, norm_mix, norm_ffn, ab_w_in, ab_gate_bias, ab_q_gain, ab_k_gain, ab_rel_bias, ab_ml_gain, ab_w_out, c_w_in, c_gate_up, c_gate_bias, c_head_gain, c_w_out, ffn_w_up, ffn_conv_w, ffn_conv_b, ffn_w_down):
    batch, seq, d = x.shape
    d_ff = ffn_w_down.shape[1]
    assert seq % (NA_UNROLL * NA_GROUP * GRID_W) == 0 and seq // GRID_W >= NA_UNION and d % LANES == 0
    tp = BLK + seq
    nb = tp // BLK
    n = batch * tp
    tm_proj = _pick_tile(tp, (1280, 1024, 768, 512, 256))
    tm_out = _pick_tile(tp, (640, 512, 256))
    tm_ffn = _pick_tile(tp, (832, 640, 512, 256))
    tm_ffn_real = _pick_tile(seq, (1024, 512, 256))
    tn_proj = 1024
    tf = _pick_tile(d_ff, (512, 256, 128))

    head = jnp.concatenate([jnp.zeros((PAD, d), x.dtype), meta_tokens.astype(x.dtype)], axis=0)

    n_main0 = 3 * NA_WIDTH + 2 * ML_QK_WIDTH + 2 * ML_WIDTH
    w_in = ab_w_in[0]
    ones = lambda k: jnp.ones((k,), F32)
    colscale0 = jnp.concatenate([
        jnp.tile(ab_q_gain[0].astype(F32), NA_HEADS), jnp.tile(ab_k_gain[0].astype(F32), NA_HEADS),
        ones(NA_WIDTH), ones(ML_QK_WIDTH), jnp.full((ML_QK_WIDTH,), ML_QK ** -0.5, F32),
        ones(2 * ML_WIDTH)])[None]
    proj0, graw = _in_proj(x, head, norm_mix[0][None].astype(F32), w_in, colscale0,
                           w_in[:, n_main0:].T.astype(BF16), n_rms=2 * NA_WIDTH // tn_proj, tail_nt=True,
                           tm=tm_proj, tn=tn_proj, tp=tp)
    gr, gc = _mlstm_gates(graw, ab_gate_bias[0].astype(F32)[:, None], tp, tm_proj)
    y_na = _na_attention(proj0, _na_bias_tables(ab_rel_bias[0], seq // GRID_W), batch, tp)
    y_f, y_b = _mlstm(proj0, gr, gc, batch, nb)
    h = _out_proj0(y_na, y_f, y_b, proj0, ab_ml_gain[0][None].astype(F32), ab_w_out[0].astype(BF16),
                   x, head, tp, tm_out)
    ffn_params = (norm_ffn[:, None].astype(F32), ffn_w_up.astype(BF16), ffn_conv_w.astype(F32),
                  ffn_conv_b[:, None].astype(F32), ffn_w_down.astype(BF16))
    h = _ffn(h, 0, *ffn_params, tp, tm_ffn, tf, False)

    n_main1 = 2 * GLA_KW + 2 * GLA_VW
    w_in = c_w_in[0]
    colscale1 = jnp.concatenate([jnp.full((GLA_KW,), GLA_DK ** -0.5, F32), ones(n_main1 - GLA_KW)])[None]
    w_tail = jnp.pad(w_in[:, n_main1:], ((0, 0), (0, LANES - 2 * GLA_RANK))).astype(BF16)
    proj1, lr = _in_proj(h, head, norm_mix[1][None].astype(F32), w_in, colscale1, w_tail,
                         n_rms=0, tail_nt=False, tm=tm_proj, tn=tn_proj)
    gu = c_gate_up[0].astype(BF16).reshape(2, GLA_RANK, GLA_HEADS, GLA_DK).transpose(0, 2, 1, 3)
    gate_up = jnp.stack([jnp.pad(gu[0], ((0, 0), (0, LANES - GLA_RANK), (0, 0))),
                         jnp.pad(gu[1], ((0, 0), (GLA_RANK, LANES - 2 * GLA_RANK), (0, 0)))])
    gate_bias = c_gate_bias[0].astype(F32).reshape(2, GLA_HEADS, 1, GLA_DK)
    o_f, o_b = _gla(proj1, lr, gate_up, gate_bias, batch, nb)
    h = _out_proj1(o_f, o_b, proj1, c_head_gain[0][None].astype(F32), c_w_out[0].astype(BF16), h, tp, tm_out)
    out = _ffn(h, 1, *ffn_params, tp, tm_ffn_real, tf, True)
    return out.reshape(batch, seq, d)
```

```python
import functools
import math

import numpy as np
import jax
import jax.numpy as jnp
from jax import lax
from jax.experimental import pallas as pl
from jax.experimental.pallas import tpu as pltpu

F32 = jnp.float32
BF16 = jnp.bfloat16
EPS = 1e-6

N_META = 16
GRID_W = 64
NA_HEADS = 8
NA_DIM = 128
NA_WIDTH = NA_HEADS * NA_DIM
WIN_H = 8
WIN_W = 16
ML_HEADS = 4
ML_QK = 128
ML_V = 256
ML_QK_WIDTH = ML_HEADS * ML_QK
ML_WIDTH = ML_HEADS * ML_V
N_ML_GATES = 4 * ML_HEADS
GLA_HEADS = 4
GLA_DK = 256
GLA_DV = 512
GLA_KW = GLA_HEADS * GLA_DK
GLA_VW = GLA_HEADS * GLA_DV
GLA_RANK = 16
GLA_TAU = 16.0
GLA_CHUNK = 64
CONV_W = 3

LANES = 128
BLK = 256
PAD = BLK - N_META
NA_GROUP = 4
NA_UNION = NA_GROUP + WIN_H - 1
NA_UNROLL = 4
HALO = 16
FFN_SUBTILES = 2
GLA_GROUP = 4
VMEM_LIMIT = 56 * 1024 * 1024

_NT = (((1,), (1,)), ((), ()))
_TN = (((0,), (0,)), ((), ()))


def _log_sigmoid(x):
    return jnp.minimum(x, 0.0) - jnp.log(1.0 + jnp.exp(-jnp.abs(x)))


def _sigmoid(x):
    return 1.0 / (1.0 + jnp.exp(-x))


def _pick_tile(n, prefs):
    for t in prefs:
        if n % t == 0:
            return t
    raise ValueError(f"no tile in {prefs} divides {n}")


def _params(sem, vmem=VMEM_LIMIT):
    return pltpu.CompilerParams(dimension_semantics=sem, vmem_limit_bytes=vmem)


def _rms_rows(x_ref, gain_ref, dst_ref, dst_row0, n_rows, rc):
    g = gain_ref[...]

    def body(r, carry):
        r0 = pl.multiple_of(r * rc, rc)
        xx = x_ref[pl.ds(r0, rc), :]
        ms = jnp.mean(xx * xx, axis=-1, keepdims=True)
        dst_ref[pl.ds(dst_row0 + r0, rc), :] = (xx * lax.rsqrt(ms + EPS) * g).astype(dst_ref.dtype)
        return carry

    lax.fori_loop(0, n_rows // rc, body, 0)


def _row_valid(tile_idx, tiles_per_batch, tm):
    it = tile_idx % tiles_per_batch
    row = lax.broadcasted_iota(jnp.int32, (tm, 1), 0)
    return (row + it * tm) >= PAD


def _in_proj_kernel(x_ref, head_ref, gain_ref, w_ref, cs_ref, wt_ref, out_ref, tail_ref, xn_ref, *,
                    n_rms, n_tail, tail_nt, tiles_per_batch, rc):
    j = pl.program_id(1)
    tm = x_ref.shape[0]
    tn = w_ref.shape[1]

    @pl.when(j == 0)
    def _():
        if tiles_per_batch is None:
            _rms_rows(x_ref, gain_ref, xn_ref, 0, tm, rc)
        else:
            first = pl.program_id(0) % tiles_per_batch == 0

            @pl.when(first)
            def _():
                _rms_rows(head_ref, gain_ref, xn_ref, 0, BLK, rc)
                _rms_rows(x_ref, gain_ref, xn_ref, BLK, tm - BLK, rc)

            @pl.when(jnp.logical_not(first))
            def _():
                _rms_rows(x_ref, gain_ref, xn_ref, 0, tm, rc)

        tail = jnp.dot(xn_ref[...], wt_ref[...], preferred_element_type=F32)
        if tail_nt:
            tail_ref[...] = tail.T[0:n_tail]
        else:
            tail_ref[...] = tail.astype(tail_ref.dtype)

    @pl.when(j > 0)
    def _():
        acc = jnp.dot(xn_ref[...], w_ref[...], preferred_element_type=F32)
        cs = cs_ref[...]
        if n_rms > 0:
            @pl.when(j <= n_rms)
            def _():
                for hh in range(tn // LANES):
                    sl = slice(hh * LANES, (hh + 1) * LANES)
                    a = acc[:, sl]
                    ms = jnp.mean(a * a, axis=-1, keepdims=True)
                    out_ref[:, sl] = (a * lax.rsqrt(ms + EPS) * cs[:, sl]).astype(out_ref.dtype)

            @pl.when(j > n_rms)
            def _():
                out_ref[...] = (acc * cs).astype(out_ref.dtype)
        else:
            out_ref[...] = (acc * cs).astype(out_ref.dtype)


def _stream_window_spec(tp, tm, d):
    tiles_per_batch = tp // tm
    align = math.gcd(tm, BLK)

    def index_map(i, *_):
        start = jnp.maximum((i % tiles_per_batch) * tm - BLK, 0)
        return i // tiles_per_batch, pl.multiple_of(start, align), 0

    return pl.BlockSpec((None, pl.Element(tm), pl.Element(d)), index_map)


def _in_proj(h, head, gain, w, colscale, *, n_rms, n_tail, tail_nt, tm, tn, tp=None):
    if tp is None:
        n, d = h.shape
        x_spec = pl.BlockSpec((tm, d), lambda i, j: (i, 0))
    else:
        d = h.shape[2]
        n = h.shape[0] * tp
        x_spec = _stream_window_spec(tp, tm, d)
    width_main = colscale.shape[1]
    n_main = width_main // tn
    assert w.shape[2] == width_main + LANES and 0 < n_tail < LANES
    wt_spec = pl.BlockSpec((None, d, LANES), lambda i, j: (0, 0, width_main // LANES))
    if tail_nt:
        tail_shape = jax.ShapeDtypeStruct((n_tail, n), F32)
        tail_spec = pl.BlockSpec((n_tail, tm), lambda i, j: (0, i))
    else:
        tail_shape = jax.ShapeDtypeStruct((n, LANES), BF16)
        tail_spec = pl.BlockSpec((tm, LANES), lambda i, j: (i, 0))
    col = lambda i, j: (0, jnp.maximum(j - 1, 0))
    wcol = lambda i, j: (0, 0, jnp.maximum(j - 1, 0))
    kern = functools.partial(_in_proj_kernel, n_rms=n_rms, n_tail=n_tail, tail_nt=tail_nt,
                             tiles_per_batch=None if tp is None else tp // tm, rc=128)
    return pl.pallas_call(
        kern,
        grid=(n // tm, n_main + 1),
        in_specs=[
            x_spec,
            pl.BlockSpec((BLK, d), lambda i, j: (0, 0)),
            pl.BlockSpec((1, d), lambda i, j: (0, 0)),
            pl.BlockSpec((None, d, tn), wcol),
            pl.BlockSpec((1, tn), col),
            wt_spec,
        ],
        out_specs=[
            pl.BlockSpec((tm, tn), lambda i, j: (i, jnp.maximum(j - 1, 0))),
            tail_spec,
        ],
        out_shape=[jax.ShapeDtypeStruct((n, width_main), BF16), tail_shape],
        scratch_shapes=[pltpu.VMEM((tm, d), BF16)],
        compiler_params=_params(("parallel", "arbitrary")),
    )(h, head, gain, w, colscale, w)


def _mlstm_gates_kernel(graw_ref, bias_ref, gr_ref, gc_ref, *, tiles_per_batch):
    width = graw_ref.shape[1]
    g = graw_ref[...] + bias_ref[...]
    row = lax.broadcasted_iota(jnp.int32, g.shape, 0)
    pos = lax.broadcasted_iota(jnp.int32, g.shape, 1)
    lane = pos & (BLK - 1)
    is_forget = (row & 4) != 0
    valid = (pos + (pl.program_id(0) % tiles_per_batch) * width) >= PAD
    lf = jnp.where(valid, _log_sigmoid(g), 0.0)
    pre = lf
    suf = lf
    d = 1
    while d < BLK:
        pre = pre + jnp.where(lane >= d, pltpu.roll(pre, d, axis=1), 0.0)
        suf = suf + jnp.where(lane < BLK - d, pltpu.roll(suf, width - d, axis=1), 0.0)
        d *= 2
    ig = jnp.where(valid, g, -jnp.inf)
    out = jnp.where(is_forget, jnp.where(row >= 8, suf, pre), ig)
    a = out - pltpu.roll(out, N_ML_GATES - 4, axis=0)
    pmax = a
    smax = a
    d = 1
    while d < BLK:
        pmax = jnp.maximum(pmax, jnp.where(lane >= d, pltpu.roll(pmax, d, axis=1), -jnp.inf))
        smax = jnp.maximum(smax, jnp.where(lane < BLK - d, pltpu.roll(smax, width - d, axis=1), -jnp.inf))
        d *= 2
    runmax = jnp.where(is_forget, 0.0, jnp.where(row >= 8, smax, pmax))
    both = jnp.concatenate([out, runmax], axis=0)
    gr_ref[...] = both
    full = jnp.concatenate([both, jnp.zeros((LANES - 2 * N_ML_GATES, width), F32)], axis=0)
    gc_ref[...] = full.T


def _mlstm_gates(graw, bias, tp, width):
    n = graw.shape[1]
    return pl.pallas_call(
        functools.partial(_mlstm_gates_kernel, tiles_per_batch=tp // width),
        grid=(n // width,),
        in_specs=[pl.BlockSpec((N_ML_GATES, width), lambda i: (0, i)),
                  pl.BlockSpec((N_ML_GATES, 1), lambda i: (0, 0))],
        out_specs=[pl.BlockSpec((2 * N_ML_GATES, width), lambda i: (0, i)),
                   pl.BlockSpec((width, LANES), lambda i: (i, 0))],
        out_shape=[jax.ShapeDtypeStruct((2 * N_ML_GATES, n), F32),
                   jax.ShapeDtypeStruct((n, LANES), F32)],
        compiler_params=_params(("parallel",)),
    )(graw, bias)


def _na_bias_tables(rel_bias, rows):
    a = np.arange(NA_GROUP)[:, None]
    u = np.arange(NA_UNION)[None, :]
    half = WIN_H // 2
    vis_first = (u < WIN_H) & (a >= 0)
    dr_first = u - a + WIN_H - 1
    vis_mid = (u >= a) & (u < a + WIN_H)
    dr_mid = u - a + WIN_H - 1 - half
    lo = NA_UNION - WIN_H
    vis_last = (u >= lo) & (a >= 0)
    dr_last = u - a + (WIN_H - 1) - (NA_UNION - NA_GROUP)
    vis_r = np.stack([vis_first, vis_mid, vis_last])
    dr = np.clip(np.stack([dr_first, dr_mid, dr_last]), 0, 2 * WIN_H - 2)
    qc = np.arange(GRID_W)[:, None]
    kc = np.arange(GRID_W)[None, :]
    c0 = np.clip(qc - WIN_W // 2, 0, GRID_W - WIN_W)
    vis_c = (kc >= c0) & (kc < c0 + WIN_W)
    dc = np.clip(kc - qc + WIN_W - 1, 0, 2 * WIN_W - 2)
    sel_c = (dc[..., None] == np.arange(2 * WIN_W - 1)).astype(np.float32)
    cols = jnp.einsum('hrd,qkd->hrqk', rel_bias.astype(F32), sel_c, precision=lax.Precision.HIGHEST)
    cols = jnp.where(vis_c, cols, -jnp.inf)
    hidden = jnp.full((NA_HEADS, GRID_W, GRID_W), -jnp.inf, F32)
    return jnp.stack([
        jnp.concatenate([
            jnp.concatenate([cols[:, dr[c, qa, ku]] if vis_r[c, qa, ku] else hidden
                             for ku in range(NA_UNION)], axis=2)
            for qa in range(NA_GROUP)], axis=1)
        for c in range(3)])


def _na_kernel(q_ref, k_ref, v_ref, bias_ref, out_ref, *, rows):
    n_groups = rows // NA_GROUP
    gq = NA_GROUP * GRID_W
    uk = NA_UNION * GRID_W
    scale = NA_DIM ** -0.5
    km = k_ref[PAD:BLK, :]
    vm = v_ref[PAD:BLK, :]

    def attend(q, kw, vw, bias):
        s_m = lax.dot_general(q, km, _NT, preferred_element_type=F32) * scale
        m = jnp.max(s_m, axis=-1, keepdims=True)
        if kw is not None:
            s = lax.dot_general(q, kw, _NT, preferred_element_type=F32) * scale + bias
            m = jnp.maximum(m, jnp.max(s, axis=-1, keepdims=True))
        p_m = jnp.exp(s_m - m)
        den = jnp.sum(p_m, axis=-1, keepdims=True)
        y = jnp.dot(p_m.astype(BF16), vm, preferred_element_type=F32)
        if kw is not None:
            p = jnp.exp(s - m)
            den = den + jnp.sum(p, axis=-1, keepdims=True)
            y = y + jnp.dot(p.astype(BF16), vw, preferred_element_type=F32)
        return y / den

    out_ref[0:PAD, :] = jnp.zeros((PAD, NA_DIM), out_ref.dtype)
    out_ref[PAD:BLK, :] = attend(q_ref[PAD:BLK, :], None, None, None).astype(out_ref.dtype)

    def group(g):
        q0 = pl.multiple_of(BLK + g * gq, gq)
        u0 = jnp.clip(g * NA_GROUP - WIN_H // 2, 0, rows - NA_UNION)
        k0 = pl.multiple_of(BLK + u0 * GRID_W, GRID_W)
        cls = jnp.where(g == 0, 0, jnp.where(g == n_groups - 1, 2, 1))
        return attend(q_ref[pl.ds(q0, gq), :], k_ref[pl.ds(k0, uk), :], v_ref[pl.ds(k0, uk), :],
                      bias_ref[cls])

    def body(gg, carry):
        g = gg * NA_UNROLL
        y = jnp.concatenate([group(g + u) for u in range(NA_UNROLL)], axis=0)
        out_ref[pl.ds(pl.multiple_of(BLK + g * gq, gq), NA_UNROLL * gq), :] = y.astype(out_ref.dtype)
        return carry

    lax.fori_loop(0, n_groups // NA_UNROLL, body, 0)


def _na_attention(proj, bias, batch, tp):
    rows = (tp - BLK) // GRID_W
    n = proj.shape[0]
    gq = NA_GROUP * GRID_W
    uk = NA_UNION * GRID_W
    return pl.pallas_call(
        functools.partial(_na_kernel, rows=rows),
        grid=(batch, NA_HEADS),
        in_specs=[
            pl.BlockSpec((tp, NA_DIM), lambda b, h: (b, h)),
            pl.BlockSpec((tp, NA_DIM), lambda b, h: (b, NA_HEADS + h)),
            pl.BlockSpec((tp, NA_DIM), lambda b, h: (b, 2 * NA_HEADS + h)),
            pl.BlockSpec((3, None, gq, uk), lambda b, h: (0, h, 0, 0)),
        ],
        out_specs=pl.BlockSpec((tp, NA_DIM), lambda b, h: (b, h)),
        out_shape=jax.ShapeDtypeStruct((n, NA_WIDTH), BF16),
        compiler_params=_params(("parallel", "parallel")),
    )(proj, proj, proj, bias)


def _mlstm_dir(q_ref, k_ref, v_ref, gr_ref, gc_ref, y_ref, c_ref, m_ref, head, rev):
    q = q_ref[:, head * ML_QK:(head + 1) * ML_QK]
    k = k_ref[:, head * ML_QK:(head + 1) * ML_QK]
    v = v_ref[:, head * ML_V:(head + 1) * ML_V]
    base = 8 if rev else 0
    a_r = gr_ref[base + head:base + head + 1, :] - gr_ref[base + 4 + head:base + 5 + head, :]
    a_c = gc_ref[:, base + head:base + head + 1] - gc_ref[:, base + 4 + head:base + 5 + head]
    b_c = gc_ref[:, base + 4 + head:base + 5 + head]
    run_c = gc_ref[:, N_ML_GATES + base + head:N_ML_GATES + base + head + 1]
    m = m_ref[head]
    m_rel = jnp.broadcast_to(jnp.maximum(m, run_c), (BLK, LANES))
    ti = lax.broadcasted_iota(jnp.int32, (BLK, BLK), 0)
    si = lax.broadcasted_iota(jnp.int32, (BLK, BLK), 1)
    mask = (si >= ti) if rev else (si <= ti)
    arg = jnp.where(mask, a_r - jnp.concatenate([m_rel] * (BLK // LANES), axis=1), -jnp.inf)
    p = jnp.exp(arg) * lax.dot_general(q, k, _NT, preferred_element_type=F32)
    w_prev = jnp.exp(m - m_rel)
    v_ext = jnp.concatenate([v, jnp.ones((BLK, LANES), BF16)], axis=1)
    c_ext = c_ref[head]
    num = (jnp.concatenate([w_prev] * (c_ext.shape[1] // LANES), axis=1)
           * jnp.dot(q, c_ext.astype(BF16), preferred_element_type=F32)
           + jnp.dot(p.astype(BF16), v_ext, preferred_element_type=F32))
    den = num[:, ML_V:]
    inv = 1.0 / jnp.maximum(jnp.abs(den), jnp.exp(-(jnp.broadcast_to(b_c, (BLK, LANES)) + m_rel)))
    y_ref[:, head * ML_V:(head + 1) * ML_V] = (
        num[:, :ML_V] * jnp.concatenate([inv] * (ML_V // LANES), axis=1)).astype(y_ref.dtype)
    last = 0 if rev else BLK - 1
    b_end = gr_ref[base + 4 + head:base + 5 + head, last:last + 1]
    m_end = jnp.maximum(m, gr_ref[N_ML_GATES + base + head:N_ML_GATES + base + head + 1, last:last + 1])
    w_tok = jnp.exp(a_c - m_end)
    c_ref[head] = jnp.exp(m - m_end) * c_ext + lax.dot_general(
        k, (w_tok * v_ext.astype(F32)).astype(BF16), _TN, preferred_element_type=F32)
    m_ref[head] = b_end + m_end


def _mlstm_kernel(qf, kf, vf, grf, gcf, qb, kb, vb, grb, gcb, yf, yb, cf, mf, cb, mb):
    @pl.when(pl.program_id(1) == 0)
    def _():
        cf[...] = jnp.zeros(cf.shape, F32)
        cb[...] = jnp.zeros(cb.shape, F32)
        mf[...] = jnp.zeros(mf.shape, F32)
        mb[...] = jnp.zeros(mb.shape, F32)

    for head in range(ML_HEADS):
        _mlstm_dir(qf, kf, vf, grf, gcf, yf, cf, mf, head, False)
        _mlstm_dir(qb, kb, vb, grb, gcb, yb, cb, mb, head, True)


def _mlstm(proj, gr, gc, batch, nb):
    n = proj.shape[0]
    q0 = 3 * NA_WIDTH // ML_QK_WIDTH
    k0 = q0 + 1
    v0 = (3 * NA_WIDTH + 2 * ML_QK_WIDTH) // ML_WIDTH
    fwd = lambda b, s: b * nb + s
    bwd = lambda b, s: b * nb + nb - 1 - s

    def specs(pos):
        return [
            pl.BlockSpec((BLK, ML_QK_WIDTH), lambda b, s: (pos(b, s), q0)),
            pl.BlockSpec((BLK, ML_QK_WIDTH), lambda b, s: (pos(b, s), k0)),
            pl.BlockSpec((BLK, ML_WIDTH), lambda b, s: (pos(b, s), v0)),
            pl.BlockSpec((2 * N_ML_GATES, BLK), lambda b, s: (0, pos(b, s))),
            pl.BlockSpec((BLK, LANES), lambda b, s: (pos(b, s), 0)),
        ]

    state = [pltpu.VMEM((ML_HEADS, ML_QK, ML_V + LANES), F32), pltpu.VMEM((ML_HEADS, 1, 1), F32)]
    return pl.pallas_call(
        _mlstm_kernel,
        grid=(batch, nb),
        in_specs=specs(fwd) + specs(bwd),
        out_specs=[pl.BlockSpec((BLK, ML_WIDTH), lambda b, s: (fwd(b, s), 0)),
                   pl.BlockSpec((BLK, ML_WIDTH), lambda b, s: (bwd(b, s), 0))],
        out_shape=[jax.ShapeDtypeStruct((n, ML_WIDTH), BF16)] * 2,
        scratch_shapes=state + state,
        compiler_params=_params(("parallel", "arbitrary")),
    )(proj, proj, proj, gr, gc, proj, proj, proj, gr, gc)


def _gla_dir(q_ref, k_ref, v_ref, lr_ref, gu_ref, gb_ref, tri_ref, o_ref, st_ref, valid, head, rev):
    dk = GLA_DK
    ksl = slice(head * GLA_DK, (head + 1) * GLA_DK)
    vsl = slice(head * GLA_DV, (head + 1) * GLA_DV)
    z = jnp.dot(lr_ref[...], gu_ref[head], preferred_element_type=F32) + gb_ref[head]
    la = jnp.where(valid, _log_sigmoid(z) * (1.0 / GLA_TAU), 0.0)
    t1 = la.astype(BF16)
    t2 = (la - t1.astype(F32)).astype(BF16)
    tri = tri_ref[...]
    b = jnp.dot(tri, t1, preferred_element_type=F32) + jnp.dot(tri, t2, preferred_element_type=F32)
    q = q_ref[:, ksl].astype(F32)
    k = k_ref[:, ksl].astype(F32)
    b_end = b[0:1] if rev else b[BLK - 1:BLK]
    q_in = (q * jnp.exp(b)).astype(BF16)
    k_dec = (k * jnp.exp(b_end - b)).astype(BF16)
    sub = GLA_CHUNK
    ti = lax.broadcasted_iota(jnp.int32, (sub, BLK), 0)
    si = lax.broadcasted_iota(jnp.int32, (sub, BLK), 1)

    def at_rows(x, row0):
        parts = [jnp.zeros((n, dk), BF16) for n in (row0,) if n] + [x]
        after = BLK - row0 - x.shape[0]
        if after:
            parts.append(jnp.zeros((after, dk), BF16))
        return jnp.concatenate(parts, axis=0) if len(parts) > 1 else x

    a_rows = []
    for i in range(BLK // sub):
        lo, hi = i * sub, (i + 1) * sub
        b_i, q_i = b[lo:hi], q[lo:hi]
        mu = b[lo + sub // 2:lo + sub // 2 + 1]
        lhs = (q_i * jnp.exp(b_i - mu)).astype(BF16)
        rhs = at_rows((k[lo:hi] * jnp.exp(mu - b_i)).astype(BF16), lo)
        n_prev = BLK - hi if rev else lo
        if n_prev:
            prev = slice(hi, BLK) if rev else slice(0, lo)
            rho = b[hi:hi + 1] if rev else b[lo - 1:lo]
            k_prev = at_rows((k[prev] * jnp.exp(rho - b[prev])).astype(BF16), hi if rev else 0)
            lhs = jnp.concatenate([(q_i * jnp.exp(b_i - rho)).astype(BF16), lhs], axis=1)
            rhs = jnp.concatenate([k_prev, rhs], axis=1)
        a_i = lax.dot_general(lhs, rhs, _NT, preferred_element_type=F32)
        seen = (si >= ti + lo) if rev else (si <= ti + lo)
        a_rows.append(jnp.where(seen, a_i, 0.0).astype(BF16))
    a = jnp.concatenate(a_rows, axis=0)
    st = st_ref[head]
    v = v_ref[:, vsl]
    o_ref[:, vsl] = (lax.dot_general(q_in, st.astype(BF16), _NT, preferred_element_type=F32)
                     + jnp.dot(a, v, preferred_element_type=F32)).astype(o_ref.dtype)
    st_ref[head] = st * jnp.exp(b_end) + lax.dot_general(v, k_dec, _TN, preferred_element_type=F32)


def _gla_kernel(qf, kf, vf, lrf, guf, gbf, bdf, qb, kb, vb, lrb, gub, gbb, bdb, of, ob, sf, sb, *, nb):
    s = pl.program_id(2)

    @pl.when(s == 0)
    def _():
        sf[...] = jnp.zeros(sf.shape, F32)
        sb[...] = jnp.zeros(sb.shape, F32)

    row = lax.broadcasted_iota(jnp.int32, (BLK, 1), 0)
    for head in range(GLA_GROUP):
        _gla_dir(qf, kf, vf, lrf, guf, gbf, bdf, of, sf, (row + s * BLK) >= PAD, head, False)
        _gla_dir(qb, kb, vb, lrb, gub, gbb, bdb, ob, sb, (row + (nb - 1 - s) * BLK) >= PAD, head, True)


def _gla(proj, lr, gate_up, gate_bias, batch, nb):
    n = proj.shape[0]
    grp = GLA_GROUP
    k0 = GLA_KW // (grp * GLA_DK)
    v0 = 2 * GLA_KW // (grp * GLA_DV)
    fwd = lambda b, h, s: b * nb + s
    bwd = lambda b, h, s: b * nb + nb - 1 - s
    t = np.arange(BLK)
    bd_f = jnp.asarray(t[None, :] <= t[:, None], BF16)
    bd_b = jnp.asarray(t[None, :] >= t[:, None], BF16)

    def specs(pos, direction):
        return [
            pl.BlockSpec((BLK, grp * GLA_DK), lambda b, h, s: (pos(b, h, s), h)),
            pl.BlockSpec((BLK, grp * GLA_DK), lambda b, h, s: (pos(b, h, s), k0 + h)),
            pl.BlockSpec((BLK, grp * GLA_DV), lambda b, h, s: (pos(b, h, s), v0 + h)),
            pl.BlockSpec((BLK, LANES), lambda b, h, s: (pos(b, h, s), 0)),
            pl.BlockSpec((None, grp, LANES, GLA_DK), lambda b, h, s: (direction, h, 0, 0)),
            pl.BlockSpec((None, grp, 1, GLA_DK), lambda b, h, s: (direction, h, 0, 0)),
            pl.BlockSpec((BLK, BLK), lambda b, h, s: (0, 0)),
        ]

    state = pltpu.VMEM((grp, GLA_DV, GLA_DK), F32)
    return pl.pallas_call(
        functools.partial(_gla_kernel, nb=nb),
        grid=(batch, GLA_HEADS // grp, nb),
        in_specs=specs(fwd, 0) + specs(bwd, 1),
        out_specs=[pl.BlockSpec((BLK, grp * GLA_DV), lambda b, h, s: (fwd(b, h, s), h)),
                   pl.BlockSpec((BLK, grp * GLA_DV), lambda b, h, s: (bwd(b, h, s), h))],
        out_shape=[jax.ShapeDtypeStruct((n, GLA_VW), BF16)] * 2,
        scratch_shapes=[state, state],
        compiler_params=_params(("parallel", "parallel", "arbitrary")),
    )(proj, proj, proj, lr, gate_up, gate_bias, bd_f, proj, proj, proj, lr, gate_up, gate_bias, bd_b)


def _gated_head_rms(yf_ref, yb_ref, gate_ref, gain_ref, n_heads, act):
    y = yf_ref[...].astype(F32) + yb_ref[...].astype(F32)
    gate = act(gate_ref[...].astype(F32))
    gain = gain_ref[...]
    hd = y.shape[1] // n_heads
    outs = []
    for hh in range(n_heads):
        sl = slice(hh * hd, (hh + 1) * hd)
        a = y[:, sl]
        ms = jnp.mean(a * a, axis=-1, keepdims=True)
        outs.append((a * lax.rsqrt(ms + EPS) * gain[:, sl] * gate[:, sl]).astype(BF16))
    return jnp.concatenate(outs, axis=1)


def _out_proj0_kernel(ya_ref, yf_ref, yb_ref, gate_ref, gain_ref, wa_ref, wb_ref, x_ref, head_ref,
                      out_ref, *, tiles_per_batch):
    tm = out_ref.shape[0]
    y_ml = _gated_head_rms(yf_ref, yb_ref, gate_ref, gain_ref, ML_HEADS, _sigmoid)
    mix = (jnp.dot(ya_ref[...], wa_ref[...], preferred_element_type=F32)
           + jnp.dot(y_ml, wb_ref[...], preferred_element_type=F32))
    first = pl.program_id(0) % tiles_per_batch == 0

    @pl.when(first)
    def _():
        row = lax.broadcasted_iota(jnp.int32, (BLK, 1), 0)
        out_ref[0:BLK, :] = jnp.where(row >= PAD, head_ref[...] + mix[0:BLK], 0.0)
        if tm > BLK:
            out_ref[BLK:, :] = x_ref[0:tm - BLK, :] + mix[BLK:]

    @pl.when(jnp.logical_not(first))
    def _():
        out_ref[...] = x_ref[...] + mix


def _out_proj1_kernel(yf_ref, yb_ref, gate_ref, gain_ref, w_ref, h_ref, out_ref, *, tiles_per_batch):
    tm = h_ref.shape[0]
    y = _gated_head_rms(yf_ref, yb_ref, gate_ref, gain_ref, GLA_HEADS, lambda r: r * _sigmoid(r))
    out = h_ref[...] + jnp.dot(y, w_ref[...], preferred_element_type=F32)
    out_ref[...] = jnp.where(_row_valid(pl.program_id(0), tiles_per_batch, tm), out, 0.0)


def _out_proj0(y_na, y_f, y_b, proj, gain, w, x, head, tp, tm):
    d = x.shape[2]
    n = x.shape[0] * tp
    assert NA_WIDTH == ML_WIDTH
    gate_blk = (3 * NA_WIDTH + 2 * ML_QK_WIDTH + ML_WIDTH) // ML_WIDTH
    row = lambda i: (i, 0)
    const = lambda i: (0, 0)
    return pl.pallas_call(
        functools.partial(_out_proj0_kernel, tiles_per_batch=tp // tm),
        grid=(n // tm,),
        in_specs=[
            pl.BlockSpec((tm, NA_WIDTH), row),
            pl.BlockSpec((tm, ML_WIDTH), row),
            pl.BlockSpec((tm, ML_WIDTH), row),
            pl.BlockSpec((tm, ML_WIDTH), lambda i: (i, gate_blk)),
            pl.BlockSpec((1, ML_WIDTH), const),
            pl.BlockSpec((NA_WIDTH, d), const, pipeline_mode=pl.Buffered(1)),
            pl.BlockSpec((ML_WIDTH, d), lambda i: (1, 0), pipeline_mode=pl.Buffered(1)),
            _stream_window_spec(tp, tm, d),
            pl.BlockSpec((BLK, d), const),
        ],
        out_specs=pl.BlockSpec((tm, d), row),
        out_shape=jax.ShapeDtypeStruct((n, d), F32),
        compiler_params=_params(("parallel",)),
    )(y_na, y_f, y_b, proj, gain, w, w, x, head)


def _out_proj1(o_f, o_b, proj, gain, w, h, tp, tm):
    n, d = h.shape
    gate_blk = (2 * GLA_KW + GLA_VW) // GLA_VW
    row = lambda i: (i, 0)
    const = lambda i: (0, 0)
    return pl.pallas_call(
        functools.partial(_out_proj1_kernel, tiles_per_batch=tp // tm),
        grid=(n // tm,),
        in_specs=[
            pl.BlockSpec((tm, GLA_VW), row),
            pl.BlockSpec((tm, GLA_VW), row),
            pl.BlockSpec((tm, GLA_VW), lambda i: (i, gate_blk)),
            pl.BlockSpec((1, GLA_VW), const),
            pl.BlockSpec((GLA_VW, d), const, pipeline_mode=pl.Buffered(1)),
            pl.BlockSpec((tm, d), row),
        ],
        out_specs=pl.BlockSpec((tm, d), row),
        out_shape=jax.ShapeDtypeStruct((n, d), F32),
        compiler_params=_params(("parallel",)),
    )(o_f, o_b, proj, gain, w, h)


def _ffn_kernel(h_ref, hp_ref, hn_ref, gain_ref, wg_ref, wv_ref, cwg_ref, cwv_ref, cbg_ref, cbv_ref,
                wd_ref, out_ref, xn_ref, *, n_sub, tiles_per_batch, real_only, rc):
    i = pl.program_id(0)
    c = pl.program_id(1)
    tm = h_ref.shape[0]

    @pl.when(c == 0)
    def _():
        _rms_rows(h_ref, gain_ref, xn_ref, HALO, tm, rc)
        _rms_rows(hp_ref, gain_ref, xn_ref, 0, HALO, HALO)
        _rms_rows(hn_ref, gain_ref, xn_ref, HALO + tm, HALO, HALO)

        if not real_only:
            @pl.when(i == 0)
            def _():
                xn_ref[0:HALO, :] = jnp.zeros((HALO, xn_ref.shape[1]), xn_ref.dtype)

        @pl.when(i == pl.num_programs(0) - 1)
        def _():
            xn_ref[HALO + tm:, :] = jnp.zeros((HALO, xn_ref.shape[1]), xn_ref.dtype)

        out_ref[...] = h_ref[...]

    def conv(row0, rows, w_ref, cw_ref, cb_ref):
        u = jnp.dot(xn_ref[row0:row0 + rows + 2 * HALO, :], w_ref[...],
                    preferred_element_type=F32)
        cw = cw_ref[...]
        out = cb_ref[...]
        for tap in range(CONV_W):
            off = HALO - CONV_W // 2 + tap
            out = out + u[off:off + rows] * cw[tap:tap + 1]
        return out

    rows = tm // n_sub
    acts = []
    for s in range(n_sub):
        g = conv(s * rows, rows, wg_ref, cwg_ref, cbg_ref)
        val = conv(s * rows, rows, wv_ref, cwv_ref, cbv_ref)
        acts.append((g * _sigmoid(g) * val).astype(BF16))
    for s in range(n_sub):
        sl = slice(s * rows, (s + 1) * rows)
        out_ref[sl, :] += jnp.dot(acts[s], wd_ref[...], preferred_element_type=F32)

    if not real_only:
        @pl.when(jnp.logical_and(c == pl.num_programs(1) - 1, i % tiles_per_batch == 0))
        def _():
            out_ref[0:PAD, :] = jnp.zeros((PAD, out_ref.shape[1]), F32)


def _ffn(h, layer, gain, w_up, conv_w, conv_b, w_down, tp, tm, tf, real_only):
    n, d = h.shape
    d_ff = w_down.shape[1]
    n_ff = d_ff // tf
    if real_only:
        tiles_per_batch = (tp - BLK) // tm
        n_out = n // tp * (tp - BLK)
        row0 = lambda i: (i // tiles_per_batch) * tp + BLK + (i % tiles_per_batch) * tm
        window = lambda rows, start: pl.BlockSpec((pl.Element(rows), pl.Element(d)),
                                                  lambda i, c: (pl.multiple_of(start(i), HALO), 0))
        h_specs = [window(tm, row0), window(HALO, lambda i: row0(i) - HALO),
                   window(HALO, lambda i: jnp.minimum(row0(i) + tm, n - HALO))]
    else:
        tiles_per_batch = tp // tm
        n_out = n
        hb = tm // HALO
        last_halo = n // HALO - 1
        h_specs = [pl.BlockSpec((tm, d), lambda i, c: (i, 0)),
                   pl.BlockSpec((HALO, d), lambda i, c: (jnp.maximum(i * hb - 1, 0), 0)),
                   pl.BlockSpec((HALO, d), lambda i, c: (jnp.minimum((i + 1) * hb, last_halo), 0))]
    return pl.pallas_call(
        functools.partial(_ffn_kernel, n_sub=FFN_SUBTILES, tiles_per_batch=tiles_per_batch,
                          real_only=real_only, rc=_pick_tile(tm, (128, 64))),
        grid=(n_out // tm, n_ff),
        in_specs=h_specs + [
            pl.BlockSpec((None, 1, d), lambda i, c: (layer, 0, 0)),
            pl.BlockSpec((None, d, tf), lambda i, c: (layer, 0, c)),
            pl.BlockSpec((None, d, tf), lambda i, c: (layer, 0, n_ff + c)),
            pl.BlockSpec((None, CONV_W, tf), lambda i, c: (layer, 0, c)),
            pl.BlockSpec((None, CONV_W, tf), lambda i, c: (layer, 0, n_ff + c)),
            pl.BlockSpec((None, 1, tf), lambda i, c: (layer, 0, c)),
            pl.BlockSpec((None, 1, tf), lambda i, c: (layer, 0, n_ff + c)),
            pl.BlockSpec((None, tf, d), lambda i, c: (layer, c, 0)),
        ],
        out_specs=pl.BlockSpec((tm, d), lambda i, c: (i, 0)),
        out_shape=jax.ShapeDtypeStruct((n_out, d), F32),
        scratch_shapes=[pltpu.VMEM((tm + 2 * HALO, d), BF16)],
        compiler_params=_params(("parallel", "arbitrary")),
    )(h, h, h, gain, w_up, w_up, conv_w, conv_w, conv_b, conv_b, w_down)


def kernel(x, meta_tokens, norm_mix, norm_ffn, ab_w_in, ab_gate_bias, ab_q_gain, ab_k_gain, ab_rel_bias, ab_ml_gain, ab_w_out, c_w_in, c_gate_up, c_gate_bias, c_head_gain, c_w_out, ffn_w_up, ffn_conv_w, ffn_conv_b, ffn_w_down):
    batch, seq, d = x.shape
    d_ff = ffn_w_down.shape[1]
    assert seq % (NA_UNROLL * NA_GROUP * GRID_W) == 0 and seq // GRID_W >= NA_UNION and d % LANES == 0
    tp = BLK + seq
    nb = tp // BLK
    n = batch * tp
    tm_proj = _pick_tile(tp, (1280, 1024, 768, 512, 256))
    tm_out = _pick_tile(tp, (640, 512, 256))
    tm_ffn = _pick_tile(tp, (832, 640, 512, 256))
    tm_ffn_real = _pick_tile(seq, (1024, 512, 256))
    tn_proj = 1024
    tf = _pick_tile(d_ff, (512, 256, 128))

    head = jnp.concatenate([jnp.zeros((PAD, d), x.dtype), meta_tokens.astype(x.dtype)], axis=0)

    ones = lambda k: jnp.ones((k,), F32)
    colscale0 = jnp.concatenate([
        jnp.tile(ab_q_gain[0].astype(F32), NA_HEADS), jnp.tile(ab_k_gain[0].astype(F32), NA_HEADS),
        ones(NA_WIDTH), ones(ML_QK_WIDTH), jnp.full((ML_QK_WIDTH,), ML_QK ** -0.5, F32),
        ones(2 * ML_WIDTH)])[None]
    lane_pad = lambda w, n_tail: jnp.pad(w, ((0, 0), (0, 0), (0, LANES - n_tail))).astype(BF16)
    proj0, graw = _in_proj(x, head, norm_mix[0][None].astype(F32), lane_pad(ab_w_in, N_ML_GATES), colscale0,
                           n_rms=2 * NA_WIDTH // tn_proj, n_tail=N_ML_GATES, tail_nt=True,
                           tm=tm_proj, tn=tn_proj, tp=tp)
    gr, gc = _mlstm_gates(graw, ab_gate_bias[0].astype(F32)[:, None], tp, tm_proj)
    y_na = _na_attention(proj0, _na_bias_tables(ab_rel_bias[0], seq // GRID_W), batch, tp)
    y_f, y_b = _mlstm(proj0, gr, gc, batch, nb)
    h = _out_proj0(y_na, y_f, y_b, proj0, ab_ml_gain[0][None].astype(F32), ab_w_out[0].astype(BF16),
                   x, head, tp, tm_out)
    ffn_params = (norm_ffn[:, None].astype(F32), ffn_w_up.astype(BF16), ffn_conv_w.astype(F32),
                  ffn_conv_b[:, None].astype(F32), ffn_w_down.astype(BF16))
    h = _ffn(h, 0, *ffn_params, tp, tm_ffn, tf, False)

    n_main1 = 2 * GLA_KW + 2 * GLA_VW
    colscale1 = jnp.concatenate([jnp.full((GLA_KW,), GLA_DK ** -0.5, F32), ones(n_main1 - GLA_KW)])[None]
    proj1, lr = _in_proj(h, head, norm_mix[1][None].astype(F32), lane_pad(c_w_in, 2 * GLA_RANK), colscale1,
                         n_rms=0, n_tail=2 * GLA_RANK, tail_nt=False, tm=tm_proj, tn=tn_proj)
    gu = c_gate_up[0].astype(BF16).reshape(2, GLA_RANK, GLA_HEADS, GLA_DK).transpose(0, 2, 1, 3)
    gate_up = jnp.stack([jnp.pad(gu[0], ((0, 0), (0, LANES - GLA_RANK), (0, 0))),
                         jnp.pad(gu[1], ((0, 0), (GLA_RANK, LANES - 2 * GLA_RANK), (0, 0)))])
    gate_bias = c_gate_bias[0].astype(F32).reshape(2, GLA_HEADS, 1, GLA_DK)
    o_f, o_b = _gla(proj1, lr, gate_up, gate_bias, batch, nb)
    h = _out_proj1(o_f, o_b, proj1, c_head_gain[0][None].astype(F32), c_w_out[0].astype(BF16), h, tp, tm_out)
    out = _ffn(h, 1, *ffn_params, tp, tm_ffn_real, tf, True)
    return out.reshape(batch, seq, d)
```

```python
import functools
import math

import numpy as np
import jax
import jax.numpy as jnp
from jax import lax
from jax.experimental import pallas as pl
from jax.experimental.pallas import tpu as pltpu

F32 = jnp.float32
BF16 = jnp.bfloat16
EPS = 1e-6

N_META = 16
GRID_W = 64
NA_HEADS = 8
NA_DIM = 128
NA_WIDTH = NA_HEADS * NA_DIM
WIN_H = 8
WIN_W = 16
ML_HEADS = 4
ML_QK = 128
ML_V = 256
ML_QK_WIDTH = ML_HEADS * ML_QK
ML_WIDTH = ML_HEADS * ML_V
N_ML_GATES = 4 * ML_HEADS
GLA_HEADS = 4
GLA_DK = 256
GLA_DV = 512
GLA_KW = GLA_HEADS * GLA_DK
GLA_VW = GLA_HEADS * GLA_DV
GLA_RANK = 16
GLA_TAU = 16.0
GLA_CHUNK = 64
CONV_W = 3

LANES = 128
SUBLANES = 8
BLK = 256
PAD = BLK - N_META
NA_GROUP = 4
NA_UNION = NA_GROUP + WIN_H - 1
NA_UNROLL = 4
HALO = 16
FFN_SUBTILES = 2
GLA_GROUP = 4
VMEM_LIMIT = 56 * 1024 * 1024

_NT = (((1,), (1,)), ((), ()))
_TN = (((0,), (0,)), ((), ()))


def _log_sigmoid(x):
    return jnp.minimum(x, 0.0) - jnp.log(1.0 + jnp.exp(-jnp.abs(x)))


def _sigmoid(x):
    return 1.0 / (1.0 + jnp.exp(-x))


def _pick_tile(n, prefs):
    for t in prefs:
        if n % t == 0:
            return t
    raise ValueError(f"no tile in {prefs} divides {n}")


def _params(sem, vmem=VMEM_LIMIT):
    return pltpu.CompilerParams(dimension_semantics=sem, vmem_limit_bytes=vmem)


def _rms_rows(x_ref, gain_ref, dst_ref, dst_row0, n_rows, rc):
    g = gain_ref[...]

    def body(r, carry):
        r0 = pl.multiple_of(r * rc, rc)
        xx = x_ref[pl.ds(r0, rc), :]
        ms = jnp.mean(xx * xx, axis=-1, keepdims=True)
        dst_ref[pl.ds(dst_row0 + r0, rc), :] = (xx * lax.rsqrt(ms + EPS) * g).astype(dst_ref.dtype)
        return carry

    lax.fori_loop(0, n_rows // rc, body, 0)


def _in_proj_kernel(x_ref, head_ref, gain_ref, w_ref, cs_ref, wt_ref, out_ref, tail_ref, xn_ref, *,
                    n_rms, n_tail, tail_nt, tiles_per_batch, rc):
    j = pl.program_id(1)
    tm = x_ref.shape[0]
    tn = w_ref.shape[1]

    @pl.when(j == 0)
    def _():
        first = pl.program_id(0) % tiles_per_batch == 0

        @pl.when(first)
        def _():
            _rms_rows(head_ref, gain_ref, xn_ref, 0, BLK, rc)
            _rms_rows(x_ref, gain_ref, xn_ref, BLK, tm - BLK, rc)

        @pl.when(jnp.logical_not(first))
        def _():
            _rms_rows(x_ref, gain_ref, xn_ref, 0, tm, rc)

        tail = jnp.dot(xn_ref[...], wt_ref[...], preferred_element_type=F32)
        if tail_nt:
            tail_ref[...] = tail.T[0:n_tail]
        else:
            tail_ref[...] = tail.astype(tail_ref.dtype)

    @pl.when(j > 0)
    def _():
        acc = jnp.dot(xn_ref[...], w_ref[...], preferred_element_type=F32)
        cs = cs_ref[...]
        if n_rms > 0:
            @pl.when(j <= n_rms)
            def _():
                for hh in range(tn // LANES):
                    sl = slice(hh * LANES, (hh + 1) * LANES)
                    a = acc[:, sl]
                    ms = jnp.mean(a * a, axis=-1, keepdims=True)
                    out_ref[:, sl] = (a * lax.rsqrt(ms + EPS) * cs[:, sl]).astype(out_ref.dtype)

            @pl.when(j > n_rms)
            def _():
                out_ref[...] = (acc * cs).astype(out_ref.dtype)
        else:
            out_ref[...] = (acc * cs).astype(out_ref.dtype)


def _stream_window_spec(tp, tm, d):
    tiles_per_batch = tp // tm
    align = math.gcd(tm, BLK)

    def index_map(i, *_):
        start = jnp.maximum((i % tiles_per_batch) * tm - BLK, 0)
        return i // tiles_per_batch, pl.multiple_of(start, align), 0

    return pl.BlockSpec((None, pl.Element(tm), pl.Element(d)), index_map)


def _head_spec(head, tp, tm):
    tiles_per_batch = tp // tm
    per_batch = head.shape[0] > 1
    return pl.BlockSpec((None, BLK, head.shape[2]),
                        lambda i, *_: (i // tiles_per_batch if per_batch else 0, 0, 0))


def _in_proj(h, head, gain, w, colscale, *, n_rms, n_tail, tail_nt, tm, tn, tp):
    d = h.shape[2]
    n = h.shape[0] * tp
    width_main = colscale.shape[1]
    n_main = width_main // tn
    assert w.shape[2] == width_main + LANES and 0 < n_tail < LANES
    wt_spec = pl.BlockSpec((None, d, LANES), lambda i, j: (0, 0, width_main // LANES))
    if tail_nt:
        tail_shape = jax.ShapeDtypeStruct((n_tail, n), F32)
        tail_spec = pl.BlockSpec((n_tail, tm), lambda i, j: (0, i))
    else:
        tail_shape = jax.ShapeDtypeStruct((n, LANES), BF16)
        tail_spec = pl.BlockSpec((tm, LANES), lambda i, j: (i, 0))
    col = lambda i, j: (0, jnp.maximum(j - 1, 0))
    wcol = lambda i, j: (0, 0, jnp.maximum(j - 1, 0))
    kern = functools.partial(_in_proj_kernel, n_rms=n_rms, n_tail=n_tail, tail_nt=tail_nt,
                             tiles_per_batch=tp // tm, rc=128)
    return pl.pallas_call(
        kern,
        grid=(n // tm, n_main + 1),
        in_specs=[
            _stream_window_spec(tp, tm, d),
            _head_spec(head, tp, tm),
            pl.BlockSpec((1, d), lambda i, j: (0, 0)),
            pl.BlockSpec((None, d, tn), wcol),
            pl.BlockSpec((1, tn), col),
            wt_spec,
        ],
        out_specs=[
            pl.BlockSpec((tm, tn), lambda i, j: (i, jnp.maximum(j - 1, 0))),
            tail_spec,
        ],
        out_shape=[jax.ShapeDtypeStruct((n, width_main), BF16), tail_shape],
        scratch_shapes=[pltpu.VMEM((tm, d), BF16)],
        compiler_params=_params(("parallel", "arbitrary")),
    )(h, head, gain, w, colscale, w)


def _mlstm_gates_kernel(graw_ref, bias_ref, gr_ref, gc_ref, *, tiles_per_batch):
    width = graw_ref.shape[1]
    g = graw_ref[...] + bias_ref[...]
    row = lax.broadcasted_iota(jnp.int32, g.shape, 0)
    pos = lax.broadcasted_iota(jnp.int32, g.shape, 1)
    lane = pos & (BLK - 1)
    is_forget = (row & 4) != 0
    valid = (pos + (pl.program_id(0) % tiles_per_batch) * width) >= PAD
    lf = jnp.where(valid, _log_sigmoid(g), 0.0)
    pre = lf
    suf = lf
    d = 1
    while d < BLK:
        pre = pre + jnp.where(lane >= d, pltpu.roll(pre, d, axis=1), 0.0)
        suf = suf + jnp.where(lane < BLK - d, pltpu.roll(suf, width - d, axis=1), 0.0)
        d *= 2
    ig = jnp.where(valid, g, -jnp.inf)
    out = jnp.where(is_forget, jnp.where(row >= 8, suf, pre), ig)
    a = out - pltpu.roll(out, N_ML_GATES - 4, axis=0)
    pmax = a
    smax = a
    d = 1
    while d < BLK:
        pmax = jnp.maximum(pmax, jnp.where(lane >= d, pltpu.roll(pmax, d, axis=1), -jnp.inf))
        smax = jnp.maximum(smax, jnp.where(lane < BLK - d, pltpu.roll(smax, width - d, axis=1), -jnp.inf))
        d *= 2
    runmax = jnp.where(is_forget, 0.0, jnp.where(row >= 8, smax, pmax))
    both = jnp.concatenate([out, runmax], axis=0)
    gr_ref[...] = both
    full = jnp.concatenate([both, jnp.zeros((LANES - 2 * N_ML_GATES, width), F32)], axis=0)
    gc_ref[...] = full.T


def _mlstm_gates(graw, bias, tp, width):
    n = graw.shape[1]
    return pl.pallas_call(
        functools.partial(_mlstm_gates_kernel, tiles_per_batch=tp // width),
        grid=(n // width,),
        in_specs=[pl.BlockSpec((N_ML_GATES, width), lambda i: (0, i)),
                  pl.BlockSpec((N_ML_GATES, 1), lambda i: (0, 0))],
        out_specs=[pl.BlockSpec((2 * N_ML_GATES, width), lambda i: (0, i)),
                   pl.BlockSpec((width, LANES), lambda i: (i, 0))],
        out_shape=[jax.ShapeDtypeStruct((2 * N_ML_GATES, n), F32),
                   jax.ShapeDtypeStruct((n, LANES), F32)],
        compiler_params=_params(("parallel",)),
    )(graw, bias)


def _na_bias_tables(rel_bias, rows):
    a = np.arange(NA_GROUP)[:, None]
    u = np.arange(NA_UNION)[None, :]
    half = WIN_H // 2
    vis_first = (u < WIN_H) & (a >= 0)
    dr_first = u - a + WIN_H - 1
    vis_mid = (u >= a) & (u < a + WIN_H)
    dr_mid = u - a + WIN_H - 1 - half
    lo = NA_UNION - WIN_H
    vis_last = (u >= lo) & (a >= 0)
    dr_last = u - a + (WIN_H - 1) - (NA_UNION - NA_GROUP)
    vis_r = np.stack([vis_first, vis_mid, vis_last])
    dr = np.clip(np.stack([dr_first, dr_mid, dr_last]), 0, 2 * WIN_H - 2)
    qc = np.arange(GRID_W)[:, None]
    kc = np.arange(GRID_W)[None, :]
    c0 = np.clip(qc - WIN_W // 2, 0, GRID_W - WIN_W)
    vis_c = (kc >= c0) & (kc < c0 + WIN_W)
    dc = np.clip(kc - qc + WIN_W - 1, 0, 2 * WIN_W - 2)
    sel_c = (dc[..., None] == np.arange(2 * WIN_W - 1)).astype(np.float32)
    cols = jnp.einsum('hrd,qkd->hrqk', rel_bias.astype(F32), sel_c, precision=lax.Precision.HIGHEST)
    cols = jnp.where(vis_c, cols, -jnp.inf)
    hidden = jnp.full((NA_HEADS, GRID_W, GRID_W), -jnp.inf, F32)
    return jnp.stack([
        jnp.concatenate([
            jnp.concatenate([cols[:, dr[c, qa, ku]] if vis_r[c, qa, ku] else hidden
                             for ku in range(NA_UNION)], axis=2)
            for qa in range(NA_GROUP)], axis=1)
        for c in range(3)])


def _na_kernel(q_ref, k_ref, v_ref, bias_ref, out_ref, *, rows):
    n_groups = rows // NA_GROUP
    gq = NA_GROUP * GRID_W
    uk = NA_UNION * GRID_W
    scale = NA_DIM ** -0.5
    km = k_ref[PAD:BLK, :]
    vm = v_ref[PAD:BLK, :]

    def attend(q, kw, vw, bias):
        s_m = lax.dot_general(q, km, _NT, preferred_element_type=F32) * scale
        m = jnp.max(s_m, axis=-1, keepdims=True)
        if kw is not None:
            s = lax.dot_general(q, kw, _NT, preferred_element_type=F32) * scale + bias
            m = jnp.maximum(m, jnp.max(s, axis=-1, keepdims=True))
        p_m = jnp.exp(s_m - m)
        den = jnp.sum(p_m, axis=-1, keepdims=True)
        y = jnp.dot(p_m.astype(BF16), vm, preferred_element_type=F32)
        if kw is not None:
            p = jnp.exp(s - m)
            den = den + jnp.sum(p, axis=-1, keepdims=True)
            y = y + jnp.dot(p.astype(BF16), vw, preferred_element_type=F32)
        return y / den

    out_ref[0:PAD, :] = jnp.zeros((PAD, NA_DIM), out_ref.dtype)
    out_ref[PAD:BLK, :] = attend(q_ref[PAD:BLK, :], None, None, None).astype(out_ref.dtype)

    def group(g):
        q0 = pl.multiple_of(BLK + g * gq, gq)
        u0 = jnp.clip(g * NA_GROUP - WIN_H // 2, 0, rows - NA_UNION)
        k0 = pl.multiple_of(BLK + u0 * GRID_W, GRID_W)
        cls = jnp.where(g == 0, 0, jnp.where(g == n_groups - 1, 2, 1))
        return attend(q_ref[pl.ds(q0, gq), :], k_ref[pl.ds(k0, uk), :], v_ref[pl.ds(k0, uk), :],
                      bias_ref[cls])

    def body(gg, carry):
        g = gg * NA_UNROLL
        y = jnp.concatenate([group(g + u) for u in range(NA_UNROLL)], axis=0)
        out_ref[pl.ds(pl.multiple_of(BLK + g * gq, gq), NA_UNROLL * gq), :] = y.astype(out_ref.dtype)
        return carry

    lax.fori_loop(0, n_groups // NA_UNROLL, body, 0)


def _na_attention(proj, bias, batch, tp):
    rows = (tp - BLK) // GRID_W
    n = proj.shape[0]
    gq = NA_GROUP * GRID_W
    uk = NA_UNION * GRID_W
    return pl.pallas_call(
        functools.partial(_na_kernel, rows=rows),
        grid=(batch, NA_HEADS),
        in_specs=[
            pl.BlockSpec((tp, NA_DIM), lambda b, h: (b, h)),
            pl.BlockSpec((tp, NA_DIM), lambda b, h: (b, NA_HEADS + h)),
            pl.BlockSpec((tp, NA_DIM), lambda b, h: (b, 2 * NA_HEADS + h)),
            pl.BlockSpec((3, None, gq, uk), lambda b, h: (0, h, 0, 0)),
        ],
        out_specs=pl.BlockSpec((tp, NA_DIM), lambda b, h: (b, h)),
        out_shape=jax.ShapeDtypeStruct((n, NA_WIDTH), BF16),
        compiler_params=_params(("parallel", "parallel")),
    )(proj, proj, proj, bias)


def _mlstm_dir(q_ref, k_ref, v_ref, gr_ref, gc_ref, y_ref, c_ref, m_ref, head, rev):
    q = q_ref[:, head * ML_QK:(head + 1) * ML_QK]
    k = k_ref[:, head * ML_QK:(head + 1) * ML_QK]
    v = v_ref[:, head * ML_V:(head + 1) * ML_V]
    base = 8 if rev else 0
    a_r = gr_ref[base + head:base + head + 1, :] - gr_ref[base + 4 + head:base + 5 + head, :]
    a_c = gc_ref[:, base + head:base + head + 1] - gc_ref[:, base + 4 + head:base + 5 + head]
    b_c = gc_ref[:, base + 4 + head:base + 5 + head]
    run_c = gc_ref[:, N_ML_GATES + base + head:N_ML_GATES + base + head + 1]
    m = m_ref[head]
    m_rel = jnp.broadcast_to(jnp.maximum(m, run_c), (BLK, LANES))
    ti = lax.broadcasted_iota(jnp.int32, (BLK, BLK), 0)
    si = lax.broadcasted_iota(jnp.int32, (BLK, BLK), 1)
    mask = (si >= ti) if rev else (si <= ti)
    arg = jnp.where(mask, a_r - jnp.concatenate([m_rel] * (BLK // LANES), axis=1), -jnp.inf)
    p = jnp.exp(arg) * lax.dot_general(q, k, _NT, preferred_element_type=F32)
    w_prev = jnp.exp(m - m_rel)
    v_ext = jnp.concatenate([v, jnp.ones((BLK, LANES), BF16)], axis=1)
    c_ext = c_ref[head]
    num = (jnp.concatenate([w_prev] * (c_ext.shape[1] // LANES), axis=1)
           * jnp.dot(q, c_ext.astype(BF16), preferred_element_type=F32)
           + jnp.dot(p.astype(BF16), v_ext, preferred_element_type=F32))
    den = num[:, ML_V:]
    inv = 1.0 / jnp.maximum(jnp.abs(den), jnp.exp(-(jnp.broadcast_to(b_c, (BLK, LANES)) + m_rel)))
    y_ref[:, head * ML_V:(head + 1) * ML_V] = (
        num[:, :ML_V] * jnp.concatenate([inv] * (ML_V // LANES), axis=1)).astype(y_ref.dtype)
    last = 0 if rev else BLK - 1
    b_end = gr_ref[base + 4 + head:base + 5 + head, last:last + 1]
    m_end = jnp.maximum(m, gr_ref[N_ML_GATES + base + head:N_ML_GATES + base + head + 1, last:last + 1])
    w_tok = jnp.exp(a_c - m_end)
    c_ref[head] = jnp.exp(m - m_end) * c_ext + lax.dot_general(
        k, (w_tok * v_ext.astype(F32)).astype(BF16), _TN, preferred_element_type=F32)
    m_ref[head] = b_end + m_end


def _mlstm_kernel(qf, kf, vf, grf, gcf, qb, kb, vb, grb, gcb, yf, yb, cf, mf, cb, mb):
    @pl.when(pl.program_id(1) == 0)
    def _():
        cf[...] = jnp.zeros(cf.shape, F32)
        cb[...] = jnp.zeros(cb.shape, F32)
        mf[...] = jnp.zeros(mf.shape, F32)
        mb[...] = jnp.zeros(mb.shape, F32)

    for head in range(ML_HEADS):
        _mlstm_dir(qf, kf, vf, grf, gcf, yf, cf, mf, head, False)
        _mlstm_dir(qb, kb, vb, grb, gcb, yb, cb, mb, head, True)


def _mlstm(proj, gr, gc, batch, nb):
    n = proj.shape[0]
    q0 = 3 * NA_WIDTH // ML_QK_WIDTH
    k0 = q0 + 1
    v0 = (3 * NA_WIDTH + 2 * ML_QK_WIDTH) // ML_WIDTH
    fwd = lambda b, s: b * nb + s
    bwd = lambda b, s: b * nb + nb - 1 - s

    def specs(pos):
        return [
            pl.BlockSpec((BLK, ML_QK_WIDTH), lambda b, s: (pos(b, s), q0)),
            pl.BlockSpec((BLK, ML_QK_WIDTH), lambda b, s: (pos(b, s), k0)),
            pl.BlockSpec((BLK, ML_WIDTH), lambda b, s: (pos(b, s), v0)),
            pl.BlockSpec((2 * N_ML_GATES, BLK), lambda b, s: (0, pos(b, s))),
            pl.BlockSpec((BLK, LANES), lambda b, s: (pos(b, s), 0)),
        ]

    state = [pltpu.VMEM((ML_HEADS, ML_QK, ML_V + LANES), F32), pltpu.VMEM((ML_HEADS, 1, 1), F32)]
    return pl.pallas_call(
        _mlstm_kernel,
        grid=(batch, nb),
        in_specs=specs(fwd) + specs(bwd),
        out_specs=[pl.BlockSpec((BLK, ML_WIDTH), lambda b, s: (fwd(b, s), 0)),
                   pl.BlockSpec((BLK, ML_WIDTH), lambda b, s: (bwd(b, s), 0))],
        out_shape=[jax.ShapeDtypeStruct((n, ML_WIDTH), BF16)] * 2,
        scratch_shapes=state + state,
        compiler_params=_params(("parallel", "arbitrary")),
    )(proj, proj, proj, gr, gc, proj, proj, proj, gr, gc)


def _gla_dir(q_ref, k_ref, v_ref, lr_ref, gu_ref, gb_ref, tri_ref, o_ref, st_ref, valid, head, rev):
    dk = GLA_DK
    ksl = slice(head * GLA_DK, (head + 1) * GLA_DK)
    vsl = slice(head * GLA_DV, (head + 1) * GLA_DV)
    z = jnp.dot(lr_ref[...], gu_ref[head], preferred_element_type=F32) + gb_ref[head]
    la = jnp.where(valid, _log_sigmoid(z) * (1.0 / GLA_TAU), 0.0)
    t1 = la.astype(BF16)
    t2 = (la - t1.astype(F32)).astype(BF16)
    tri = tri_ref[...]
    b = jnp.dot(tri, t1, preferred_element_type=F32) + jnp.dot(tri, t2, preferred_element_type=F32)
    q = q_ref[:, ksl].astype(F32)
    k = k_ref[:, ksl].astype(F32)
    b_end = b[0:1] if rev else b[BLK - 1:BLK]
    q_in = (q * jnp.exp(b)).astype(BF16)
    k_dec = (k * jnp.exp(b_end - b)).astype(BF16)
    sub = GLA_CHUNK
    ti = lax.broadcasted_iota(jnp.int32, (sub, BLK), 0)
    si = lax.broadcasted_iota(jnp.int32, (sub, BLK), 1)

    def at_rows(x, row0):
        parts = [jnp.zeros((n, dk), BF16) for n in (row0,) if n] + [x]
        after = BLK - row0 - x.shape[0]
        if after:
            parts.append(jnp.zeros((after, dk), BF16))
        return jnp.concatenate(parts, axis=0) if len(parts) > 1 else x

    a_rows = []
    for i in range(BLK // sub):
        lo, hi = i * sub, (i + 1) * sub
        b_i, q_i = b[lo:hi], q[lo:hi]
        mu = b[lo + sub // 2:lo + sub // 2 + 1]
        lhs = (q_i * jnp.exp(b_i - mu)).astype(BF16)
        rhs = at_rows((k[lo:hi] * jnp.exp(mu - b_i)).astype(BF16), lo)
        n_prev = BLK - hi if rev else lo
        if n_prev:
            prev = slice(hi, BLK) if rev else slice(0, lo)
            rho = b[hi:hi + 1] if rev else b[lo - 1:lo]
            k_prev = at_rows((k[prev] * jnp.exp(rho - b[prev])).astype(BF16), hi if rev else 0)
            lhs = jnp.concatenate([(q_i * jnp.exp(b_i - rho)).astype(BF16), lhs], axis=1)
            rhs = jnp.concatenate([k_prev, rhs], axis=1)
        a_i = lax.dot_general(lhs, rhs, _NT, preferred_element_type=F32)
        seen = (si >= ti + lo) if rev else (si <= ti + lo)
        a_rows.append(jnp.where(seen, a_i, 0.0).astype(BF16))
    a = jnp.concatenate(a_rows, axis=0)
    st = st_ref[head]
    v = v_ref[:, vsl]
    o_ref[:, vsl] = (lax.dot_general(q_in, st.astype(BF16), _NT, preferred_element_type=F32)
                     + jnp.dot(a, v, preferred_element_type=F32)).astype(o_ref.dtype)
    st_ref[head] = st * jnp.exp(b_end) + lax.dot_general(v, k_dec, _TN, preferred_element_type=F32)


def _gla_kernel(qf, kf, vf, lrf, guf, gbf, bdf, qb, kb, vb, lrb, gub, gbb, bdb, of, ob, sf, sb, *, nb):
    s = pl.program_id(2)

    @pl.when(s == 0)
    def _():
        sf[...] = jnp.zeros(sf.shape, F32)
        sb[...] = jnp.zeros(sb.shape, F32)

    row = lax.broadcasted_iota(jnp.int32, (BLK, 1), 0)
    for head in range(GLA_GROUP):
        _gla_dir(qf, kf, vf, lrf, guf, gbf, bdf, of, sf, (row + s * BLK) >= PAD, head, False)
        _gla_dir(qb, kb, vb, lrb, gub, gbb, bdb, ob, sb, (row + (nb - 1 - s) * BLK) >= PAD, head, True)


def _gla(proj, lr, gate_up, gate_bias, batch, nb):
    n = proj.shape[0]
    grp = GLA_GROUP
    k0 = GLA_KW // (grp * GLA_DK)
    v0 = 2 * GLA_KW // (grp * GLA_DV)
    fwd = lambda b, h, s: b * nb + s
    bwd = lambda b, h, s: b * nb + nb - 1 - s
    t = np.arange(BLK)
    bd_f = jnp.asarray(t[None, :] <= t[:, None], BF16)
    bd_b = jnp.asarray(t[None, :] >= t[:, None], BF16)

    def specs(pos, direction):
        return [
            pl.BlockSpec((BLK, grp * GLA_DK), lambda b, h, s: (pos(b, h, s), h)),
            pl.BlockSpec((BLK, grp * GLA_DK), lambda b, h, s: (pos(b, h, s), k0 + h)),
            pl.BlockSpec((BLK, grp * GLA_DV), lambda b, h, s: (pos(b, h, s), v0 + h)),
            pl.BlockSpec((BLK, LANES), lambda b, h, s: (pos(b, h, s), 0)),
            pl.BlockSpec((None, grp, LANES, GLA_DK), lambda b, h, s: (direction, h, 0, 0)),
            pl.BlockSpec((None, grp, 1, GLA_DK), lambda b, h, s: (direction, h, 0, 0)),
            pl.BlockSpec((BLK, BLK), lambda b, h, s: (0, 0)),
        ]

    state = pltpu.VMEM((grp, GLA_DV, GLA_DK), F32)
    return pl.pallas_call(
        functools.partial(_gla_kernel, nb=nb),
        grid=(batch, GLA_HEADS // grp, nb),
        in_specs=specs(fwd, 0) + specs(bwd, 1),
        out_specs=[pl.BlockSpec((BLK, grp * GLA_DV), lambda b, h, s: (fwd(b, h, s), h)),
                   pl.BlockSpec((BLK, grp * GLA_DV), lambda b, h, s: (bwd(b, h, s), h))],
        out_shape=[jax.ShapeDtypeStruct((n, GLA_VW), BF16)] * 2,
        scratch_shapes=[state, state],
        compiler_params=_params(("parallel", "parallel", "arbitrary")),
    )(proj, proj, proj, lr, gate_up, gate_bias, bd_f, proj, proj, proj, lr, gate_up, gate_bias, bd_b)


def _gated_head_rms(yf_ref, yb_ref, gate_ref, gain_ref, n_heads, act):
    y = yf_ref[...].astype(F32) + yb_ref[...].astype(F32)
    gate = act(gate_ref[...].astype(F32))
    gain = gain_ref[...]
    hd = y.shape[1] // n_heads
    outs = []
    for hh in range(n_heads):
        sl = slice(hh * hd, (hh + 1) * hd)
        a = y[:, sl]
        ms = jnp.mean(a * a, axis=-1, keepdims=True)
        outs.append((a * lax.rsqrt(ms + EPS) * gain[:, sl] * gate[:, sl]).astype(BF16))
    return jnp.concatenate(outs, axis=1)


def _add_residual(out_ref, x_ref, head_ref, mix, tiles_per_batch):
    tm = out_ref.shape[0]
    first = pl.program_id(0) % tiles_per_batch == 0

    @pl.when(first)
    def _():
        row = lax.broadcasted_iota(jnp.int32, (BLK, 1), 0)
        out_ref[0:BLK, :] = jnp.where(row >= PAD, head_ref[...] + mix[0:BLK], 0.0)
        if tm > BLK:
            out_ref[BLK:, :] = x_ref[0:tm - BLK, :] + mix[BLK:]

    @pl.when(jnp.logical_not(first))
    def _():
        out_ref[...] = x_ref[...] + mix


def _out_proj0_kernel(ya_ref, yf_ref, yb_ref, gate_ref, gain_ref, wa_ref, wb_ref, x_ref, head_ref,
                      out_ref, *, tiles_per_batch):
    y_ml = _gated_head_rms(yf_ref, yb_ref, gate_ref, gain_ref, ML_HEADS, _sigmoid)
    mix = (jnp.dot(ya_ref[...], wa_ref[...], preferred_element_type=F32)
           + jnp.dot(y_ml, wb_ref[...], preferred_element_type=F32))
    _add_residual(out_ref, x_ref, head_ref, mix, tiles_per_batch)


def _out_proj1_kernel(yf_ref, yb_ref, gate_ref, gain_ref, w_ref, x_ref, head_ref, out_ref, *,
                      tiles_per_batch):
    y = _gated_head_rms(yf_ref, yb_ref, gate_ref, gain_ref, GLA_HEADS, lambda r: r * _sigmoid(r))
    mix = jnp.dot(y, w_ref[...], preferred_element_type=F32)
    _add_residual(out_ref, x_ref, head_ref, mix, tiles_per_batch)


def _out_proj0(y_na, y_f, y_b, proj, gain, w, x, head, tp, tm):
    d = x.shape[2]
    n = x.shape[0] * tp
    assert NA_WIDTH == ML_WIDTH
    gate_blk = (3 * NA_WIDTH + 2 * ML_QK_WIDTH + ML_WIDTH) // ML_WIDTH
    row = lambda i: (i, 0)
    const = lambda i: (0, 0)
    return pl.pallas_call(
        functools.partial(_out_proj0_kernel, tiles_per_batch=tp // tm),
        grid=(n // tm,),
        in_specs=[
            pl.BlockSpec((tm, NA_WIDTH), row),
            pl.BlockSpec((tm, ML_WIDTH), row),
            pl.BlockSpec((tm, ML_WIDTH), row),
            pl.BlockSpec((tm, ML_WIDTH), lambda i: (i, gate_blk)),
            pl.BlockSpec((1, ML_WIDTH), const),
            pl.BlockSpec((NA_WIDTH, d), const, pipeline_mode=pl.Buffered(1)),
            pl.BlockSpec((ML_WIDTH, d), lambda i: (1, 0), pipeline_mode=pl.Buffered(1)),
            _stream_window_spec(tp, tm, d),
            _head_spec(head, tp, tm),
        ],
        out_specs=pl.BlockSpec((tm, d), row),
        out_shape=jax.ShapeDtypeStruct((n, d), F32),
        compiler_params=_params(("parallel",)),
    )(y_na, y_f, y_b, proj, gain, w, w, x, head)


def _out_proj1(o_f, o_b, proj, gain, w, x, head, tp, tm):
    d = x.shape[2]
    n = x.shape[0] * tp
    gate_blk = (2 * GLA_KW + GLA_VW) // GLA_VW
    row = lambda i: (i, 0)
    const = lambda i: (0, 0)
    return pl.pallas_call(
        functools.partial(_out_proj1_kernel, tiles_per_batch=tp // tm),
        grid=(n // tm,),
        in_specs=[
            pl.BlockSpec((tm, GLA_VW), row),
            pl.BlockSpec((tm, GLA_VW), row),
            pl.BlockSpec((tm, GLA_VW), lambda i: (i, gate_blk)),
            pl.BlockSpec((1, GLA_VW), const),
            pl.BlockSpec((GLA_VW, d), const, pipeline_mode=pl.Buffered(1)),
            _stream_window_spec(tp, tm, d),
            _head_spec(head, tp, tm),
        ],
        out_specs=pl.BlockSpec((tm, d), row),
        out_shape=jax.ShapeDtypeStruct((n, d), F32),
        compiler_params=_params(("parallel",)),
    )(o_f, o_b, proj, gain, w, x, head)


def _ffn_kernel(h_ref, hp_ref, hn_ref, gain_ref, wg_ref, wv_ref, cpg_ref, cpv_ref, wd_ref, out_ref,
                xn_ref, *, n_sub, meta_blocks, rc):
    i = pl.program_id(0)
    c = pl.program_id(1)
    tm = h_ref.shape[0]

    @pl.when(c == 0)
    def _():
        _rms_rows(h_ref, gain_ref, xn_ref, HALO, tm, rc)
        _rms_rows(hp_ref, gain_ref, xn_ref, 0, HALO, HALO)
        _rms_rows(hn_ref, gain_ref, xn_ref, HALO + tm, HALO, HALO)

        if not meta_blocks:
            @pl.when(i == pl.num_programs(0) - 1)
            def _():
                xn_ref[HALO + tm:, :] = jnp.zeros((HALO, xn_ref.shape[1]), xn_ref.dtype)

        out_ref[...] = h_ref[...]

    def conv(row0, rows, w_ref, cp_ref):
        u = jnp.dot(xn_ref[row0:row0 + rows + 2 * HALO, :], w_ref[...],
                    preferred_element_type=F32)
        cp = cp_ref[...]
        out = cp[CONV_W:CONV_W + 1]
        for tap in range(CONV_W):
            off = HALO - CONV_W // 2 + tap
            out = out + u[off:off + rows] * cp[tap:tap + 1]
        return out

    rows = tm // n_sub
    acts = []
    for s in range(n_sub):
        g = conv(s * rows, rows, wg_ref, cpg_ref)
        val = conv(s * rows, rows, wv_ref, cpv_ref)
        acts.append((g * _sigmoid(g) * val).astype(BF16))
    for s in range(n_sub):
        sl = slice(s * rows, (s + 1) * rows)
        out_ref[sl, :] += jnp.dot(acts[s], wd_ref[...], preferred_element_type=F32)

    if meta_blocks:
        @pl.when(c == pl.num_programs(1) - 1)
        def _():
            out_ref[0:PAD, :] = jnp.zeros((PAD, out_ref.shape[1]), F32)


def _ffn(h, layer, gain, w_up, conv_p, w_down, tp, tm, tf, meta_blocks):
    n, d = h.shape
    d_ff = w_down.shape[1]
    n_ff = d_ff // tf
    if meta_blocks:
        assert tm == BLK
        n_out = n // tp * BLK
        row0 = lambda i: i * tp
    else:
        tiles_per_batch = (tp - BLK) // tm
        n_out = n // tp * (tp - BLK)
        row0 = lambda i: (i // tiles_per_batch) * tp + BLK + (i % tiles_per_batch) * tm
    window = lambda rows, start: pl.BlockSpec((pl.Element(rows), pl.Element(d)),
                                              lambda i, c: (pl.multiple_of(start(i), HALO), 0))
    h_specs = [window(tm, row0), window(HALO, lambda i: jnp.maximum(row0(i) - HALO, 0)),
               window(HALO, lambda i: jnp.minimum(row0(i) + tm, n - HALO))]
    return pl.pallas_call(
        functools.partial(_ffn_kernel, n_sub=FFN_SUBTILES, meta_blocks=meta_blocks,
                          rc=_pick_tile(tm, (128, 64))),
        grid=(n_out // tm, n_ff),
        in_specs=h_specs + [
            pl.BlockSpec((None, 1, d), lambda i, c: (layer, 0, 0)),
            pl.BlockSpec((None, d, tf), lambda i, c: (layer, 0, c)),
            pl.BlockSpec((None, d, tf), lambda i, c: (layer, 0, n_ff + c)),
            pl.BlockSpec((None, conv_p.shape[1], tf), lambda i, c: (layer, 0, c)),
            pl.BlockSpec((None, conv_p.shape[1], tf), lambda i, c: (layer, 0, n_ff + c)),
            pl.BlockSpec((None, tf, d), lambda i, c: (layer, c, 0)),
        ],
        out_specs=pl.BlockSpec((tm, d), lambda i, c: (i, 0)),
        out_shape=jax.ShapeDtypeStruct((n_out, d), F32),
        scratch_shapes=[pltpu.VMEM((tm + 2 * HALO, d), BF16)],
        compiler_params=_params(("parallel", "arbitrary")),
    )(h, h, h, gain, w_up, w_up, conv_p, conv_p, w_down)


def kernel(x, meta_tokens, norm_mix, norm_ffn, ab_w_in, ab_gate_bias, ab_q_gain, ab_k_gain, ab_rel_bias, ab_ml_gain, ab_w_out, c_w_in, c_gate_up, c_gate_bias, c_head_gain, c_w_out, ffn_w_up, ffn_conv_w, ffn_conv_b, ffn_w_down):
    batch, seq, d = x.shape
    d_ff = ffn_w_down.shape[1]
    assert seq % (NA_UNROLL * NA_GROUP * GRID_W) == 0 and seq // GRID_W >= NA_UNION and d % LANES == 0
    tp = BLK + seq
    nb = tp // BLK
    tm_proj = _pick_tile(tp, (1280, 1024, 768, 512, 256))
    tm_out = _pick_tile(tp, (640, 512, 256))
    tm_ffn = _pick_tile(seq, (1024, 512, 256))
    tn_proj = 1024
    tf = _pick_tile(d_ff, (512, 256, 128))

    head = jnp.concatenate([jnp.zeros((PAD, d), x.dtype), meta_tokens.astype(x.dtype)], axis=0)[None]

    ones = lambda k: jnp.ones((k,), F32)
    colscale0 = jnp.concatenate([
        jnp.tile(ab_q_gain[0].astype(F32), NA_HEADS), jnp.tile(ab_k_gain[0].astype(F32), NA_HEADS),
        ones(NA_WIDTH), ones(ML_QK_WIDTH), jnp.full((ML_QK_WIDTH,), ML_QK ** -0.5, F32),
        ones(2 * ML_WIDTH)])[None]
    lane_pad = lambda w, n_tail: jnp.pad(w, ((0, 0), (0, 0), (0, LANES - n_tail))).astype(BF16)
    proj0, graw = _in_proj(x, head, norm_mix[0][None].astype(F32), lane_pad(ab_w_in, N_ML_GATES), colscale0,
                           n_rms=2 * NA_WIDTH // tn_proj, n_tail=N_ML_GATES, tail_nt=True,
                           tm=tm_proj, tn=tn_proj, tp=tp)
    gr, gc = _mlstm_gates(graw, ab_gate_bias[0].astype(F32)[:, None], tp, tm_proj)
    y_na = _na_attention(proj0, _na_bias_tables(ab_rel_bias[0], seq // GRID_W), batch, tp)
    y_f, y_b = _mlstm(proj0, gr, gc, batch, nb)
    h = _out_proj0(y_na, y_f, y_b, proj0, ab_ml_gain[0][None].astype(F32), ab_w_out[0].astype(BF16),
                   x, head, tp, tm_out)
    conv_p = jnp.concatenate([ffn_conv_w.astype(F32), ffn_conv_b[:, None].astype(F32),
                              jnp.zeros((ffn_conv_w.shape[0], SUBLANES - CONV_W - 1, 2 * d_ff), F32)], axis=1)
    ffn_params = (norm_ffn[:, None].astype(F32), ffn_w_up.astype(BF16), conv_p, ffn_w_down.astype(BF16))
    x1 = _ffn(h, 0, *ffn_params, tp, tm_ffn, tf, False).reshape(batch, seq, d)
    head1 = _ffn(h, 0, *ffn_params, tp, BLK, tf, True).reshape(batch, BLK, d)

    n_main1 = 2 * GLA_KW + 2 * GLA_VW
    colscale1 = jnp.concatenate([jnp.full((GLA_KW,), GLA_DK ** -0.5, F32), ones(n_main1 - GLA_KW)])[None]
    proj1, lr = _in_proj(x1, head1, norm_mix[1][None].astype(F32), lane_pad(c_w_in, 2 * GLA_RANK), colscale1,
                         n_rms=0, n_tail=2 * GLA_RANK, tail_nt=False, tm=tm_proj, tn=tn_proj, tp=tp)
    gu = c_gate_up[0].astype(BF16).reshape(2, GLA_RANK, GLA_HEADS, GLA_DK).transpose(0, 2, 1, 3)
    gate_up = jnp.stack([jnp.pad(gu[0], ((0, 0), (0, LANES - GLA_RANK), (0, 0))),
                         jnp.pad(gu[1], ((0, 0), (GLA_RANK, LANES - 2 * GLA_RANK), (0, 0)))])
    gate_bias = c_gate_bias[0].astype(F32).reshape(2, GLA_HEADS, 1, GLA_DK)
    o_f, o_b = _gla(proj1, lr, gate_up, gate_bias, batch, nb)
    h = _out_proj1(o_f, o_b, proj1, c_head_gain[0][None].astype(F32), c_w_out[0].astype(BF16),
                   x1, head1, tp, tm_out)
    return _ffn(h, 1, *ffn_params, tp, tm_ffn, tf, False).reshape(batch, seq, d)
```

```python
import functools
import math

import numpy as np
import jax
import jax.numpy as jnp
from jax import lax
from jax.experimental import pallas as pl
from jax.experimental.pallas import tpu as pltpu

F32 = jnp.float32
BF16 = jnp.bfloat16
EPS = 1e-6

N_META = 16
GRID_W = 64
NA_HEADS = 8
NA_DIM = 128
NA_WIDTH = NA_HEADS * NA_DIM
WIN_H = 8
WIN_W = 16
ML_HEADS = 4
ML_QK = 128
ML_V = 256
ML_QK_WIDTH = ML_HEADS * ML_QK
ML_WIDTH = ML_HEADS * ML_V
N_ML_GATES = 4 * ML_HEADS
GLA_HEADS = 4
GLA_DK = 256
GLA_DV = 512
GLA_KW = GLA_HEADS * GLA_DK
GLA_VW = GLA_HEADS * GLA_DV
GLA_RANK = 16
GLA_TAU = 16.0
GLA_CHUNK = 64
CONV_W = 3

LANES = 128
SUBLANES = 8
BLK = 256
PAD = BLK - N_META
NA_GROUP = 4
NA_UNION = NA_GROUP + WIN_H - 1
NA_UNROLL = 8
HALO = 16
FFN_SUBTILES = 2
GLA_GROUP = 4
RMS_ROWS = (128, 64)
VMEM_LIMIT = 56 * 1024 * 1024

_NT = (((1,), (1,)), ((), ()))
_TN = (((0,), (0,)), ((), ()))


def _log_sigmoid(x):
    return jnp.minimum(x, 0.0) - jnp.log(1.0 + jnp.exp(-jnp.abs(x)))


def _sigmoid(x):
    return 1.0 / (1.0 + jnp.exp(-x))


def _pick_tile(n, prefs):
    for t in prefs:
        if n % t == 0:
            return t
    raise ValueError(f"no tile in {prefs} divides {n}")


def _params(sem, vmem=VMEM_LIMIT):
    return pltpu.CompilerParams(dimension_semantics=sem, vmem_limit_bytes=vmem)


def _rms_rows(x_ref, gain_ref, dst_ref, dst_row0, n_rows, rc):
    g = gain_ref[...]

    def body(r, carry):
        r0 = pl.multiple_of(r * rc, rc)
        xx = x_ref[pl.ds(r0, rc), :]
        ms = jnp.mean(xx * xx, axis=-1, keepdims=True)
        dst_ref[pl.ds(dst_row0 + r0, rc), :] = (xx * lax.rsqrt(ms + EPS) * g).astype(dst_ref.dtype)
        return carry

    lax.fori_loop(0, n_rows // rc, body, 0)


def _in_proj_kernel(x_ref, head_ref, gain_ref, w_ref, cs_ref, wt_ref, out_ref, tail_ref, xn_ref, *,
                    n_rms, n_tail, tail_nt, tiles_per_batch, rc):
    j = pl.program_id(1)
    tm = x_ref.shape[0]
    tn = w_ref.shape[1]

    @pl.when(j == 0)
    def _():
        first = pl.program_id(0) % tiles_per_batch == 0

        @pl.when(first)
        def _():
            _rms_rows(head_ref, gain_ref, xn_ref, 0, BLK, rc)
            _rms_rows(x_ref, gain_ref, xn_ref, BLK, tm - BLK, rc)

        @pl.when(jnp.logical_not(first))
        def _():
            _rms_rows(x_ref, gain_ref, xn_ref, 0, tm, rc)

        tail = jnp.dot(xn_ref[...], wt_ref[...], preferred_element_type=F32)
        if tail_nt:
            tail_ref[...] = tail.T[0:n_tail]
        else:
            tail_ref[...] = tail.astype(tail_ref.dtype)

    @pl.when(j > 0)
    def _():
        acc = jnp.dot(xn_ref[...], w_ref[...], preferred_element_type=F32)
        cs = cs_ref[...]
        if n_rms > 0:
            @pl.when(j <= n_rms)
            def _():
                for hh in range(tn // LANES):
                    sl = slice(hh * LANES, (hh + 1) * LANES)
                    a = acc[:, sl]
                    ms = jnp.mean(a * a, axis=-1, keepdims=True)
                    out_ref[:, sl] = (a * lax.rsqrt(ms + EPS) * cs[:, sl]).astype(out_ref.dtype)

            @pl.when(j > n_rms)
            def _():
                out_ref[...] = (acc * cs).astype(out_ref.dtype)
        else:
            out_ref[...] = (acc * cs).astype(out_ref.dtype)


def _stream_window_spec(tp, tm, d):
    tiles_per_batch = tp // tm
    align = math.gcd(tm, BLK)

    def index_map(i, *_):
        start = jnp.maximum((i % tiles_per_batch) * tm - BLK, 0)
        return i // tiles_per_batch, pl.multiple_of(start, align), 0

    return pl.BlockSpec((None, pl.Element(tm), pl.Element(d)), index_map)


def _head_spec(head, tp, tm):
    tiles_per_batch = tp // tm
    per_batch = head.shape[0] > 1
    return pl.BlockSpec((None, BLK, head.shape[2]),
                        lambda i, *_: (i // tiles_per_batch if per_batch else 0, 0, 0))


def _in_proj(h, head, gain, w, colscale, *, n_rms, n_tail, tail_nt, tm, tn, tp):
    d = h.shape[2]
    n = h.shape[0] * tp
    width_main = colscale.shape[1]
    n_main = width_main // tn
    assert w.shape[2] == width_main + LANES and 0 < n_tail < LANES
    wt_spec = pl.BlockSpec((None, d, LANES), lambda i, j: (0, 0, width_main // LANES))
    if tail_nt:
        tail_shape = jax.ShapeDtypeStruct((n_tail, n), F32)
        tail_spec = pl.BlockSpec((n_tail, tm), lambda i, j: (0, i))
    else:
        tail_shape = jax.ShapeDtypeStruct((n, LANES), BF16)
        tail_spec = pl.BlockSpec((tm, LANES), lambda i, j: (i, 0))
    col = lambda i, j: (0, jnp.maximum(j - 1, 0))
    wcol = lambda i, j: (0, 0, jnp.maximum(j - 1, 0))
    kern = functools.partial(_in_proj_kernel, n_rms=n_rms, n_tail=n_tail, tail_nt=tail_nt,
                             tiles_per_batch=tp // tm, rc=_pick_tile(math.gcd(tm, BLK), RMS_ROWS))
    return pl.pallas_call(
        kern,
        grid=(n // tm, n_main + 1),
        in_specs=[
            _stream_window_spec(tp, tm, d),
            _head_spec(head, tp, tm),
            pl.BlockSpec((1, d), lambda i, j: (0, 0)),
            pl.BlockSpec((None, d, tn), wcol),
            pl.BlockSpec((1, tn), col),
            wt_spec,
        ],
        out_specs=[
            pl.BlockSpec((tm, tn), lambda i, j: (i, jnp.maximum(j - 1, 0))),
            tail_spec,
        ],
        out_shape=[jax.ShapeDtypeStruct((n, width_main), BF16), tail_shape],
        scratch_shapes=[pltpu.VMEM((tm, d), BF16)],
        compiler_params=_params(("parallel", "arbitrary")),
    )(h, head, gain, w, colscale, w)


def _mlstm_gates_kernel(graw_ref, bias_ref, gr_ref, gc_ref, *, tiles_per_batch):
    width = graw_ref.shape[1]
    g = graw_ref[...] + bias_ref[...]
    row = lax.broadcasted_iota(jnp.int32, g.shape, 0)
    pos = lax.broadcasted_iota(jnp.int32, g.shape, 1)
    lane = pos & (BLK - 1)
    is_forget = (row & ML_HEADS) != 0
    is_bwd = row >= 2 * ML_HEADS
    valid = (pos + (pl.program_id(0) % tiles_per_batch) * width) >= PAD
    lf = jnp.where(valid, _log_sigmoid(g), 0.0)
    pre = lf
    suf = lf
    d = 1
    while d < BLK:
        pre = pre + jnp.where(lane >= d, pltpu.roll(pre, d, axis=1), 0.0)
        suf = suf + jnp.where(lane < BLK - d, pltpu.roll(suf, width - d, axis=1), 0.0)
        d *= 2
    ig = jnp.where(valid, g, -jnp.inf)
    out = jnp.where(is_forget, jnp.where(is_bwd, suf, pre), ig)
    a = out - pltpu.roll(out, N_ML_GATES - ML_HEADS, axis=0)
    pmax = a
    smax = a
    d = 1
    while d < BLK:
        pmax = jnp.maximum(pmax, jnp.where(lane >= d, pltpu.roll(pmax, d, axis=1), -jnp.inf))
        smax = jnp.maximum(smax, jnp.where(lane < BLK - d, pltpu.roll(smax, width - d, axis=1), -jnp.inf))
        d *= 2
    runmax = jnp.where(is_forget, 0.0, jnp.where(is_bwd, smax, pmax))
    both = jnp.concatenate([out, runmax], axis=0)
    gr_ref[...] = both
    full = jnp.concatenate([both, jnp.zeros((LANES - 2 * N_ML_GATES, width), F32)], axis=0)
    gc_ref[...] = full.T


def _mlstm_gates(graw, bias, tp, width):
    n = graw.shape[1]
    return pl.pallas_call(
        functools.partial(_mlstm_gates_kernel, tiles_per_batch=tp // width),
        grid=(n // width,),
        in_specs=[pl.BlockSpec((N_ML_GATES, width), lambda i: (0, i)),
                  pl.BlockSpec((N_ML_GATES, 1), lambda i: (0, 0))],
        out_specs=[pl.BlockSpec((2 * N_ML_GATES, width), lambda i: (0, i)),
                   pl.BlockSpec((width, LANES), lambda i: (i, 0))],
        out_shape=[jax.ShapeDtypeStruct((2 * N_ML_GATES, n), F32),
                   jax.ShapeDtypeStruct((n, LANES), F32)],
        compiler_params=_params(("parallel",)),
    )(graw, bias)


def _na_bias_tables(rel_bias, rows):
    a = np.arange(NA_GROUP)[:, None]
    u = np.arange(NA_UNION)[None, :]
    half = WIN_H // 2
    vis_first = (u < WIN_H) & (a >= 0)
    dr_first = u - a + WIN_H - 1
    vis_mid = (u >= a) & (u < a + WIN_H)
    dr_mid = u - a + WIN_H - 1 - half
    lo = NA_UNION - WIN_H
    vis_last = (u >= lo) & (a >= 0)
    dr_last = u - a + (WIN_H - 1) - (NA_UNION - NA_GROUP)
    vis_r = np.stack([vis_first, vis_mid, vis_last])
    dr = np.clip(np.stack([dr_first, dr_mid, dr_last]), 0, 2 * WIN_H - 2)
    qc = np.arange(GRID_W)[:, None]
    kc = np.arange(GRID_W)[None, :]
    c0 = np.clip(qc - WIN_W // 2, 0, GRID_W - WIN_W)
    vis_c = (kc >= c0) & (kc < c0 + WIN_W)
    dc = np.clip(kc - qc + WIN_W - 1, 0, 2 * WIN_W - 2)
    sel_c = (dc[..., None] == np.arange(2 * WIN_W - 1)).astype(np.float32)
    cols = jnp.einsum('hrd,qkd->hrqk', rel_bias.astype(F32), sel_c, precision=lax.Precision.HIGHEST)
    cols = jnp.where(vis_c, cols, -jnp.inf)
    hidden = jnp.full((NA_HEADS, GRID_W, GRID_W), -jnp.inf, F32)
    return jnp.stack([
        jnp.concatenate([
            jnp.concatenate([cols[:, dr[c, qa, ku]] if vis_r[c, qa, ku] else hidden
                             for ku in range(NA_UNION)], axis=2)
            for qa in range(NA_GROUP)], axis=1)
        for c in range(3)])


def _na_kernel(q_ref, k_ref, v_ref, bias_ref, out_ref, *, rows):
    n_groups = rows // NA_GROUP
    gq = NA_GROUP * GRID_W
    uk = NA_UNION * GRID_W
    scale = NA_DIM ** -0.5
    km = k_ref[PAD:BLK, :]
    vm = v_ref[PAD:BLK, :]

    def attend(q, kw, vw, bias):
        s_m = lax.dot_general(q, km, _NT, preferred_element_type=F32) * scale
        m = jnp.max(s_m, axis=-1, keepdims=True)
        if kw is not None:
            s = lax.dot_general(q, kw, _NT, preferred_element_type=F32) * scale + bias
            m = jnp.maximum(m, jnp.max(s, axis=-1, keepdims=True))
        p_m = jnp.exp(s_m - m)
        den = jnp.sum(p_m, axis=-1, keepdims=True)
        y = jnp.dot(p_m.astype(BF16), vm, preferred_element_type=F32)
        if kw is not None:
            p = jnp.exp(s - m)
            den = den + jnp.sum(p, axis=-1, keepdims=True)
            y = y + jnp.dot(p.astype(BF16), vw, preferred_element_type=F32)
        return y / den

    out_ref[0:PAD, :] = jnp.zeros((PAD, NA_DIM), out_ref.dtype)
    out_ref[PAD:BLK, :] = attend(q_ref[PAD:BLK, :], None, None, None).astype(out_ref.dtype)

    def group(g):
        q0 = pl.multiple_of(BLK + g * gq, gq)
        u0 = jnp.clip(g * NA_GROUP - WIN_H // 2, 0, rows - NA_UNION)
        k0 = pl.multiple_of(BLK + u0 * GRID_W, GRID_W)
        cls = jnp.where(g == 0, 0, jnp.where(g == n_groups - 1, 2, 1))
        return attend(q_ref[pl.ds(q0, gq), :], k_ref[pl.ds(k0, uk), :], v_ref[pl.ds(k0, uk), :],
                      bias_ref[cls])

    def body(gg, carry):
        g = gg * NA_UNROLL
        y = jnp.concatenate([group(g + u) for u in range(NA_UNROLL)], axis=0)
        out_ref[pl.ds(pl.multiple_of(BLK + g * gq, gq), NA_UNROLL * gq), :] = y.astype(out_ref.dtype)
        return carry

    lax.fori_loop(0, n_groups // NA_UNROLL, body, 0)


def _na_attention(proj, bias, batch, tp):
    rows = (tp - BLK) // GRID_W
    n = proj.shape[0]
    gq = NA_GROUP * GRID_W
    uk = NA_UNION * GRID_W
    return pl.pallas_call(
        functools.partial(_na_kernel, rows=rows),
        grid=(batch, NA_HEADS),
        in_specs=[
            pl.BlockSpec((tp, NA_DIM), lambda b, h: (b, h)),
            pl.BlockSpec((tp, NA_DIM), lambda b, h: (b, NA_HEADS + h)),
            pl.BlockSpec((tp, NA_DIM), lambda b, h: (b, 2 * NA_HEADS + h)),
            pl.BlockSpec((3, None, gq, uk), lambda b, h: (0, h, 0, 0)),
        ],
        out_specs=pl.BlockSpec((tp, NA_DIM), lambda b, h: (b, h)),
        out_shape=jax.ShapeDtypeStruct((n, NA_WIDTH), BF16),
        compiler_params=_params(("parallel", "parallel")),
    )(proj, proj, proj, bias)


def _mlstm_dir(q_ref, k_ref, v_ref, gr_ref, gc_ref, y_ref, c_ref, m_ref, head, rev):
    q = q_ref[:, head * ML_QK:(head + 1) * ML_QK]
    k = k_ref[:, head * ML_QK:(head + 1) * ML_QK]
    v = v_ref[:, head * ML_V:(head + 1) * ML_V]
    ig = (2 * ML_HEADS if rev else 0) + head
    b = ig + ML_HEADS
    run = N_ML_GATES + ig
    a_r = gr_ref[ig:ig + 1, :] - gr_ref[b:b + 1, :]
    a_c = gc_ref[:, ig:ig + 1] - gc_ref[:, b:b + 1]
    b_c = gc_ref[:, b:b + 1]
    run_c = gc_ref[:, run:run + 1]
    m = m_ref[head]
    m_rel = jnp.broadcast_to(jnp.maximum(m, run_c), (BLK, LANES))
    ti = lax.broadcasted_iota(jnp.int32, (BLK, BLK), 0)
    si = lax.broadcasted_iota(jnp.int32, (BLK, BLK), 1)
    mask = (si >= ti) if rev else (si <= ti)
    arg = jnp.where(mask, a_r - jnp.concatenate([m_rel] * (BLK // LANES), axis=1), -jnp.inf)
    p = jnp.exp(arg) * lax.dot_general(q, k, _NT, preferred_element_type=F32)
    w_prev = jnp.exp(m - m_rel)
    v_ext = jnp.concatenate([v, jnp.ones((BLK, LANES), BF16)], axis=1)
    c_ext = c_ref[head]
    num = (jnp.concatenate([w_prev] * (c_ext.shape[1] // LANES), axis=1)
           * jnp.dot(q, c_ext.astype(BF16), preferred_element_type=F32)
           + jnp.dot(p.astype(BF16), v_ext, preferred_element_type=F32))
    den = num[:, ML_V:]
    inv = 1.0 / jnp.maximum(jnp.abs(den), jnp.exp(-(jnp.broadcast_to(b_c, (BLK, LANES)) + m_rel)))
    y_ref[:, head * ML_V:(head + 1) * ML_V] = (
        num[:, :ML_V] * jnp.concatenate([inv] * (ML_V // LANES), axis=1)).astype(y_ref.dtype)
    last = 0 if rev else BLK - 1
    b_end = gr_ref[b:b + 1, last:last + 1]
    m_end = jnp.maximum(m, gr_ref[run:run + 1, last:last + 1])
    w_tok = jnp.exp(a_c - m_end)
    c_ref[head] = jnp.exp(m - m_end) * c_ext + lax.dot_general(
        k, (w_tok * v_ext.astype(F32)).astype(BF16), _TN, preferred_element_type=F32)
    m_ref[head] = b_end + m_end


def _mlstm_kernel(qf, kf, vf, grf, gcf, qb, kb, vb, grb, gcb, yf, yb, cf, mf, cb, mb):
    @pl.when(pl.program_id(1) == 0)
    def _():
        cf[...] = jnp.zeros(cf.shape, F32)
        cb[...] = jnp.zeros(cb.shape, F32)
        mf[...] = jnp.zeros(mf.shape, F32)
        mb[...] = jnp.zeros(mb.shape, F32)

    for head in range(ML_HEADS):
        _mlstm_dir(qf, kf, vf, grf, gcf, yf, cf, mf, head, False)
        _mlstm_dir(qb, kb, vb, grb, gcb, yb, cb, mb, head, True)


def _mlstm(proj, gr, gc, batch, nb):
    n = proj.shape[0]
    q0 = 3 * NA_WIDTH // ML_QK_WIDTH
    k0 = q0 + 1
    v0 = (3 * NA_WIDTH + 2 * ML_QK_WIDTH) // ML_WIDTH
    fwd = lambda b, s: b * nb + s
    bwd = lambda b, s: b * nb + nb - 1 - s

    def specs(pos):
        return [
            pl.BlockSpec((BLK, ML_QK_WIDTH), lambda b, s: (pos(b, s), q0)),
            pl.BlockSpec((BLK, ML_QK_WIDTH), lambda b, s: (pos(b, s), k0)),
            pl.BlockSpec((BLK, ML_WIDTH), lambda b, s: (pos(b, s), v0)),
            pl.BlockSpec((2 * N_ML_GATES, BLK), lambda b, s: (0, pos(b, s))),
            pl.BlockSpec((BLK, LANES), lambda b, s: (pos(b, s), 0)),
        ]

    state = [pltpu.VMEM((ML_HEADS, ML_QK, ML_V + LANES), F32), pltpu.VMEM((ML_HEADS, 1, 1), F32)]
    return pl.pallas_call(
        _mlstm_kernel,
        grid=(batch, nb),
        in_specs=specs(fwd) + specs(bwd),
        out_specs=[pl.BlockSpec((BLK, ML_WIDTH), lambda b, s: (fwd(b, s), 0)),
                   pl.BlockSpec((BLK, ML_WIDTH), lambda b, s: (bwd(b, s), 0))],
        out_shape=[jax.ShapeDtypeStruct((n, ML_WIDTH), BF16)] * 2,
        scratch_shapes=state + state,
        compiler_params=_params(("parallel", "arbitrary")),
    )(proj, proj, proj, gr, gc, proj, proj, proj, gr, gc)


def _gla_dir(q_ref, k_ref, v_ref, lr_ref, gu_ref, gb_ref, tri_ref, o_ref, st_ref, valid, head, rev):
    dk = GLA_DK
    ksl = slice(head * GLA_DK, (head + 1) * GLA_DK)
    vsl = slice(head * GLA_DV, (head + 1) * GLA_DV)
    z = jnp.dot(lr_ref[...], gu_ref[head], preferred_element_type=F32) + gb_ref[head]
    la = jnp.where(valid, _log_sigmoid(z) * (1.0 / GLA_TAU), 0.0)
    t1 = la.astype(BF16)
    t2 = (la - t1.astype(F32)).astype(BF16)
    tri = tri_ref[...]
    b = jnp.dot(tri, t1, preferred_element_type=F32) + jnp.dot(tri, t2, preferred_element_type=F32)
    q = q_ref[:, ksl].astype(F32)
    k = k_ref[:, ksl].astype(F32)
    b_end = b[0:1] if rev else b[BLK - 1:BLK]
    q_in = (q * jnp.exp(b)).astype(BF16)
    k_dec = (k * jnp.exp(b_end - b)).astype(BF16)
    sub = GLA_CHUNK
    ti = lax.broadcasted_iota(jnp.int32, (sub, BLK), 0)
    si = lax.broadcasted_iota(jnp.int32, (sub, BLK), 1)

    def at_rows(x, row0):
        parts = [jnp.zeros((n, dk), BF16) for n in (row0,) if n] + [x]
        after = BLK - row0 - x.shape[0]
        if after:
            parts.append(jnp.zeros((after, dk), BF16))
        return jnp.concatenate(parts, axis=0) if len(parts) > 1 else x

    a_rows = []
    for i in range(BLK // sub):
        lo, hi = i * sub, (i + 1) * sub
        b_i, q_i = b[lo:hi], q[lo:hi]
        mu = b[lo + sub // 2:lo + sub // 2 + 1]
        lhs = (q_i * jnp.exp(b_i - mu)).astype(BF16)
        rhs = at_rows((k[lo:hi] * jnp.exp(mu - b_i)).astype(BF16), lo)
        n_prev = BLK - hi if rev else lo
        if n_prev:
            prev = slice(hi, BLK) if rev else slice(0, lo)
            rho = b[hi:hi + 1] if rev else b[lo - 1:lo]
            k_prev = at_rows((k[prev] * jnp.exp(rho - b[prev])).astype(BF16), hi if rev else 0)
            lhs = jnp.concatenate([(q_i * jnp.exp(b_i - rho)).astype(BF16), lhs], axis=1)
            rhs = jnp.concatenate([k_prev, rhs], axis=1)
        a_i = lax.dot_general(lhs, rhs, _NT, preferred_element_type=F32)
        seen = (si >= ti + lo) if rev else (si <= ti + lo)
        a_rows.append(jnp.where(seen, a_i, 0.0).astype(BF16))
    a = jnp.concatenate(a_rows, axis=0)
    st = st_ref[head]
    v = v_ref[:, vsl]
    o_ref[:, vsl] = (lax.dot_general(q_in, st.astype(BF16), _NT, preferred_element_type=F32)
                     + jnp.dot(a, v, preferred_element_type=F32)).astype(o_ref.dtype)
    st_ref[head] = st * jnp.exp(b_end) + lax.dot_general(v, k_dec, _TN, preferred_element_type=F32)


def _gla_kernel(qf, kf, vf, lrf, guf, gbf, bdf, qb, kb, vb, lrb, gub, gbb, bdb, of, ob, sf, sb, *, nb):
    s = pl.program_id(2)

    @pl.when(s == 0)
    def _():
        sf[...] = jnp.zeros(sf.shape, F32)
        sb[...] = jnp.zeros(sb.shape, F32)

    row = lax.broadcasted_iota(jnp.int32, (BLK, 1), 0)
    for head in range(GLA_GROUP):
        _gla_dir(qf, kf, vf, lrf, guf, gbf, bdf, of, sf, (row + s * BLK) >= PAD, head, False)
        _gla_dir(qb, kb, vb, lrb, gub, gbb, bdb, ob, sb, (row + (nb - 1 - s) * BLK) >= PAD, head, True)


def _gla(proj, lr, gate_up, gate_bias, batch, nb):
    n = proj.shape[0]
    grp = GLA_GROUP
    k0 = GLA_KW // (grp * GLA_DK)
    v0 = 2 * GLA_KW // (grp * GLA_DV)
    fwd = lambda b, h, s: b * nb + s
    bwd = lambda b, h, s: b * nb + nb - 1 - s
    t = np.arange(BLK)
    bd_f = jnp.asarray(t[None, :] <= t[:, None], BF16)
    bd_b = jnp.asarray(t[None, :] >= t[:, None], BF16)

    def specs(pos, direction):
        return [
            pl.BlockSpec((BLK, grp * GLA_DK), lambda b, h, s: (pos(b, h, s), h)),
            pl.BlockSpec((BLK, grp * GLA_DK), lambda b, h, s: (pos(b, h, s), k0 + h)),
            pl.BlockSpec((BLK, grp * GLA_DV), lambda b, h, s: (pos(b, h, s), v0 + h)),
            pl.BlockSpec((BLK, LANES), lambda b, h, s: (pos(b, h, s), 0)),
            pl.BlockSpec((None, grp, LANES, GLA_DK), lambda b, h, s: (direction, h, 0, 0)),
            pl.BlockSpec((None, grp, 1, GLA_DK), lambda b, h, s: (direction, h, 0, 0)),
            pl.BlockSpec((BLK, BLK), lambda b, h, s: (0, 0)),
        ]

    state = pltpu.VMEM((grp, GLA_DV, GLA_DK), F32)
    return pl.pallas_call(
        functools.partial(_gla_kernel, nb=nb),
        grid=(batch, GLA_HEADS // grp, nb),
        in_specs=specs(fwd, 0) + specs(bwd, 1),
        out_specs=[pl.BlockSpec((BLK, grp * GLA_DV), lambda b, h, s: (fwd(b, h, s), h)),
                   pl.BlockSpec((BLK, grp * GLA_DV), lambda b, h, s: (bwd(b, h, s), h))],
        out_shape=[jax.ShapeDtypeStruct((n, GLA_VW), BF16)] * 2,
        scratch_shapes=[state, state],
        compiler_params=_params(("parallel", "parallel", "arbitrary")),
    )(proj, proj, proj, lr, gate_up, gate_bias, bd_f, proj, proj, proj, lr, gate_up, gate_bias, bd_b)


def _gated_head_rms(yf_ref, yb_ref, gate_ref, gain_ref, n_heads, act):
    y = yf_ref[...].astype(F32) + yb_ref[...].astype(F32)
    gate = act(gate_ref[...].astype(F32))
    gain = gain_ref[...]
    hd = y.shape[1] // n_heads
    outs = []
    for hh in range(n_heads):
        sl = slice(hh * hd, (hh + 1) * hd)
        a = y[:, sl]
        ms = jnp.mean(a * a, axis=-1, keepdims=True)
        outs.append((a * lax.rsqrt(ms + EPS) * gain[:, sl] * gate[:, sl]).astype(BF16))
    return jnp.concatenate(outs, axis=1)


def _add_residual(out_ref, x_ref, head_ref, mix, tiles_per_batch):
    tm = out_ref.shape[0]
    first = pl.program_id(0) % tiles_per_batch == 0

    @pl.when(first)
    def _():
        row = lax.broadcasted_iota(jnp.int32, (BLK, 1), 0)
        out_ref[0:BLK, :] = jnp.where(row >= PAD, head_ref[...] + mix[0:BLK], 0.0)
        if tm > BLK:
            out_ref[BLK:, :] = x_ref[0:tm - BLK, :] + mix[BLK:]

    @pl.when(jnp.logical_not(first))
    def _():
        out_ref[...] = x_ref[...] + mix


def _out_proj0_kernel(ya_ref, yf_ref, yb_ref, gate_ref, gain_ref, wa_ref, wb_ref, x_ref, head_ref,
                      out_ref, *, tiles_per_batch):
    y_ml = _gated_head_rms(yf_ref, yb_ref, gate_ref, gain_ref, ML_HEADS, _sigmoid)
    mix = (jnp.dot(ya_ref[...], wa_ref[...], preferred_element_type=F32)
           + jnp.dot(y_ml, wb_ref[...], preferred_element_type=F32))
    _add_residual(out_ref, x_ref, head_ref, mix, tiles_per_batch)


def _out_proj1_kernel(yf_ref, yb_ref, gate_ref, gain_ref, w_ref, x_ref, head_ref, out_ref, *,
                      tiles_per_batch):
    y = _gated_head_rms(yf_ref, yb_ref, gate_ref, gain_ref, GLA_HEADS, lambda r: r * _sigmoid(r))
    mix = jnp.dot(y, w_ref[...], preferred_element_type=F32)
    _add_residual(out_ref, x_ref, head_ref, mix, tiles_per_batch)


def _out_proj0(y_na, y_f, y_b, proj, gain, w, x, head, tp, tm):
    d = x.shape[2]
    n = x.shape[0] * tp
    assert NA_WIDTH == ML_WIDTH
    gate_blk = (3 * NA_WIDTH + 2 * ML_QK_WIDTH + ML_WIDTH) // ML_WIDTH
    row = lambda i: (i, 0)
    const = lambda i: (0, 0)
    return pl.pallas_call(
        functools.partial(_out_proj0_kernel, tiles_per_batch=tp // tm),
        grid=(n // tm,),
        in_specs=[
            pl.BlockSpec((tm, NA_WIDTH), row),
            pl.BlockSpec((tm, ML_WIDTH), row),
            pl.BlockSpec((tm, ML_WIDTH), row),
            pl.BlockSpec((tm, ML_WIDTH), lambda i: (i, gate_blk)),
            pl.BlockSpec((1, ML_WIDTH), const),
            pl.BlockSpec((NA_WIDTH, d), const, pipeline_mode=pl.Buffered(1)),
            pl.BlockSpec((ML_WIDTH, d), lambda i: (1, 0), pipeline_mode=pl.Buffered(1)),
            _stream_window_spec(tp, tm, d),
            _head_spec(head, tp, tm),
        ],
        out_specs=pl.BlockSpec((tm, d), row),
        out_shape=jax.ShapeDtypeStruct((n, d), F32),
        compiler_params=_params(("parallel",)),
    )(y_na, y_f, y_b, proj, gain, w, w, x, head)


def _out_proj1(o_f, o_b, proj, gain, w, x, head, tp, tm):
    d = x.shape[2]
    n = x.shape[0] * tp
    gate_blk = (2 * GLA_KW + GLA_VW) // GLA_VW
    row = lambda i: (i, 0)
    const = lambda i: (0, 0)
    return pl.pallas_call(
        functools.partial(_out_proj1_kernel, tiles_per_batch=tp // tm),
        grid=(n // tm,),
        in_specs=[
            pl.BlockSpec((tm, GLA_VW), row),
            pl.BlockSpec((tm, GLA_VW), row),
            pl.BlockSpec((tm, GLA_VW), lambda i: (i, gate_blk)),
            pl.BlockSpec((1, GLA_VW), const),
            pl.BlockSpec((GLA_VW, d), const, pipeline_mode=pl.Buffered(1)),
            _stream_window_spec(tp, tm, d),
            _head_spec(head, tp, tm),
        ],
        out_specs=pl.BlockSpec((tm, d), row),
        out_shape=jax.ShapeDtypeStruct((n, d), F32),
        compiler_params=_params(("parallel",)),
    )(o_f, o_b, proj, gain, w, x, head)


def _ffn_kernel(h_ref, hp_ref, hn_ref, gain_ref, wg_ref, wv_ref, cpg_ref, cpv_ref, wd_ref, out_ref,
                xn_ref, *, n_sub, meta_blocks, rc):
    i = pl.program_id(0)
    c = pl.program_id(1)
    tm = h_ref.shape[0]

    @pl.when(c == 0)
    def _():
        _rms_rows(h_ref, gain_ref, xn_ref, HALO, tm, rc)
        _rms_rows(hp_ref, gain_ref, xn_ref, 0, HALO, HALO)
        _rms_rows(hn_ref, gain_ref, xn_ref, HALO + tm, HALO, HALO)

        if not meta_blocks:
            @pl.when(i == pl.num_programs(0) - 1)
            def _():
                xn_ref[HALO + tm:, :] = jnp.zeros((HALO, xn_ref.shape[1]), xn_ref.dtype)

        out_ref[...] = h_ref[...]

    def conv(row0, rows, w_ref, cp_ref):
        u = jnp.dot(xn_ref[row0:row0 + rows + 2 * HALO, :], w_ref[...],
                    preferred_element_type=F32)
        cp = cp_ref[...]
        out = cp[CONV_W:CONV_W + 1]
        for tap in range(CONV_W):
            off = HALO - CONV_W // 2 + tap
            out = out + u[off:off + rows] * cp[tap:tap + 1]
        return out

    rows = tm // n_sub
    acts = []
    for s in range(n_sub):
        g = conv(s * rows, rows, wg_ref, cpg_ref)
        val = conv(s * rows, rows, wv_ref, cpv_ref)
        acts.append((g * _sigmoid(g) * val).astype(BF16))
    for s in range(n_sub):
        sl = slice(s * rows, (s + 1) * rows)
        out_ref[sl, :] += jnp.dot(acts[s], wd_ref[...], preferred_element_type=F32)

    if meta_blocks:
        @pl.when(c == pl.num_programs(1) - 1)
        def _():
            out_ref[0:PAD, :] = jnp.zeros((PAD, out_ref.shape[1]), F32)


def _ffn(h, layer, gain, w_up, conv_p, w_down, tp, tm, tf, meta_blocks):
    n, d = h.shape
    d_ff = w_down.shape[1]
    n_ff = d_ff // tf
    if meta_blocks:
        assert tm == BLK
        n_out = n // tp * BLK
        row0 = lambda i: i * tp
    else:
        tiles_per_batch = (tp - BLK) // tm
        n_out = n // tp * (tp - BLK)
        row0 = lambda i: (i // tiles_per_batch) * tp + BLK + (i % tiles_per_batch) * tm
    window = lambda rows, start: pl.BlockSpec((pl.Element(rows), pl.Element(d)),
                                              lambda i, c: (pl.multiple_of(start(i), HALO), 0))
    h_specs = [window(tm, row0), window(HALO, lambda i: jnp.maximum(row0(i) - HALO, 0)),
               window(HALO, lambda i: jnp.minimum(row0(i) + tm, n - HALO))]
    return pl.pallas_call(
        functools.partial(_ffn_kernel, n_sub=FFN_SUBTILES, meta_blocks=meta_blocks,
                          rc=_pick_tile(tm, RMS_ROWS)),
        grid=(n_out // tm, n_ff),
        in_specs=h_specs + [
            pl.BlockSpec((None, 1, d), lambda i, c: (layer, 0, 0)),
            pl.BlockSpec((None, d, tf), lambda i, c: (layer, 0, c)),
            pl.BlockSpec((None, d, tf), lambda i, c: (layer, 0, n_ff + c)),
            pl.BlockSpec((None, conv_p.shape[1], tf), lambda i, c: (layer, 0, c)),
            pl.BlockSpec((None, conv_p.shape[1], tf), lambda i, c: (layer, 0, n_ff + c)),
            pl.BlockSpec((None, tf, d), lambda i, c: (layer, c, 0)),
        ],
        out_specs=pl.BlockSpec((tm, d), lambda i, c: (i, 0)),
        out_shape=jax.ShapeDtypeStruct((n_out, d), F32),
        scratch_shapes=[pltpu.VMEM((tm + 2 * HALO, d), BF16)],
        compiler_params=_params(("parallel", "arbitrary")),
    )(h, h, h, gain, w_up, w_up, conv_p, conv_p, w_down)


def kernel(x, meta_tokens, norm_mix, norm_ffn, ab_w_in, ab_gate_bias, ab_q_gain, ab_k_gain, ab_rel_bias, ab_ml_gain, ab_w_out, c_w_in, c_gate_up, c_gate_bias, c_head_gain, c_w_out, ffn_w_up, ffn_conv_w, ffn_conv_b, ffn_w_down):
    batch, seq, d = x.shape
    d_ff = ffn_w_down.shape[1]
    assert seq % (NA_UNROLL * NA_GROUP * GRID_W) == 0 and seq // GRID_W >= NA_UNION and d % LANES == 0
    tp = BLK + seq
    nb = tp // BLK
    tm_proj = _pick_tile(tp, (1280, 1024, 768, 512, 256))
    tm_out = _pick_tile(tp, (640, 512, 256))
    tm_ffn = _pick_tile(seq, (1024, 512, 256))
    tn_proj = 1024
    tf = _pick_tile(d_ff, (512, 256, 128))

    head = jnp.concatenate([jnp.zeros((PAD, d), x.dtype), meta_tokens.astype(x.dtype)], axis=0)[None]

    ones = lambda k: jnp.ones((k,), F32)
    colscale0 = jnp.concatenate([
        jnp.tile(ab_q_gain[0].astype(F32), NA_HEADS), jnp.tile(ab_k_gain[0].astype(F32), NA_HEADS),
        ones(NA_WIDTH), ones(ML_QK_WIDTH), jnp.full((ML_QK_WIDTH,), ML_QK ** -0.5, F32),
        ones(2 * ML_WIDTH)])[None]
    lane_pad = lambda w, n_tail: jnp.pad(w, ((0, 0), (0, 0), (0, LANES - n_tail))).astype(BF16)
    proj0, graw = _in_proj(x, head, norm_mix[0][None].astype(F32), lane_pad(ab_w_in, N_ML_GATES), colscale0,
                           n_rms=2 * NA_WIDTH // tn_proj, n_tail=N_ML_GATES, tail_nt=True,
                           tm=tm_proj, tn=tn_proj, tp=tp)
    gr, gc = _mlstm_gates(graw, ab_gate_bias[0].astype(F32)[:, None], tp, tm_proj)
    y_na = _na_attention(proj0, _na_bias_tables(ab_rel_bias[0], seq // GRID_W), batch, tp)
    y_f, y_b = _mlstm(proj0, gr, gc, batch, nb)
    h = _out_proj0(y_na, y_f, y_b, proj0, ab_ml_gain[0][None].astype(F32), ab_w_out[0].astype(BF16),
                   x, head, tp, tm_out)
    conv_p = jnp.concatenate([ffn_conv_w.astype(F32), ffn_conv_b[:, None].astype(F32),
                              jnp.zeros((ffn_conv_w.shape[0], SUBLANES - CONV_W - 1, 2 * d_ff), F32)], axis=1)
    ffn_params = (norm_ffn[:, None].astype(F32), ffn_w_up.astype(BF16), conv_p, ffn_w_down.astype(BF16))
    x1 = _ffn(h, 0, *ffn_params, tp, tm_ffn, tf, False).reshape(batch, seq, d)
    head1 = _ffn(h, 0, *ffn_params, tp, BLK, tf, True).reshape(batch, BLK, d)

    n_main1 = 2 * GLA_KW + 2 * GLA_VW
    colscale1 = jnp.concatenate([jnp.full((GLA_KW,), GLA_DK ** -0.5, F32), ones(n_main1 - GLA_KW)])[None]
    proj1, lr = _in_proj(x1, head1, norm_mix[1][None].astype(F32), lane_pad(c_w_in, 2 * GLA_RANK), colscale1,
                         n_rms=0, n_tail=2 * GLA_RANK, tail_nt=False, tm=tm_proj, tn=tn_proj, tp=tp)
    gu = c_gate_up[0].astype(BF16).reshape(2, GLA_RANK, GLA_HEADS, GLA_DK).transpose(0, 2, 1, 3)
    gate_up = jnp.stack([jnp.pad(gu[0], ((0, 0), (0, LANES - GLA_RANK), (0, 0))),
                         jnp.pad(gu[1], ((0, 0), (GLA_RANK, LANES - 2 * GLA_RANK), (0, 0)))])
    gate_bias = c_gate_bias[0].astype(F32).reshape(2, GLA_HEADS, 1, GLA_DK)
    o_f, o_b = _gla(proj1, lr, gate_up, gate_bias, batch, nb)
    h = _out_proj1(o_f, o_b, proj1, c_head_gain[0][None].astype(F32), c_w_out[0].astype(BF16),
                   x1, head1, tp, tm_out)
    return _ffn(h, 1, *ffn_params, tp, tm_ffn, tf, False).reshape(batch, seq, d)
```

```python
import functools
import math

import numpy as np
import jax
import jax.numpy as jnp
from jax import lax
from jax.experimental import pallas as pl
from jax.experimental.pallas import tpu as pltpu

F32 = jnp.float32
BF16 = jnp.bfloat16
EPS = 1e-6

N_META = 16
GRID_W = 64
NA_HEADS = 8
NA_DIM = 128
NA_WIDTH = NA_HEADS * NA_DIM
WIN_H = 8
WIN_W = 16
ML_HEADS = 4
ML_QK = 128
ML_V = 256
ML_QK_WIDTH = ML_HEADS * ML_QK
ML_WIDTH = ML_HEADS * ML_V
N_ML_GATES = 4 * ML_HEADS
GLA_HEADS = 4
GLA_DK = 256
GLA_DV = 512
GLA_KW = GLA_HEADS * GLA_DK
GLA_VW = GLA_HEADS * GLA_DV
GLA_RANK = 16
GLA_TAU = 16.0
GLA_CHUNK = 64
CONV_W = 3

LANES = 128
SUBLANES = 8
BLK = 256
PAD = BLK - N_META
NA_GROUP = 4
NA_UNION = NA_GROUP + WIN_H - 1
NA_UNROLL = 8
HALO = 16
FFN_SUBTILES = 2
GLA_GROUP = 4
RMS_ROWS = (128, 64)
VMEM_LIMIT = 56 * 1024 * 1024

_NT = (((1,), (1,)), ((), ()))
_TN = (((0,), (0,)), ((), ()))


def _log_sigmoid(x):
    return jnp.minimum(x, 0.0) - jnp.log(1.0 + jnp.exp(-jnp.abs(x)))


def _sigmoid(x):
    return 1.0 / (1.0 + jnp.exp(-x))


def _pick_tile(n, prefs):
    for t in prefs:
        if n % t == 0:
            return t
    raise ValueError(f"no tile in {prefs} divides {n}")


def _params(sem, vmem=VMEM_LIMIT):
    return pltpu.CompilerParams(dimension_semantics=sem, vmem_limit_bytes=vmem)


def _rms_rows(x_ref, gain_ref, dst_ref, dst_row0, n_rows, rc):
    g = gain_ref[...]

    def body(r, carry):
        r0 = pl.multiple_of(r * rc, rc)
        xx = x_ref[pl.ds(r0, rc), :]
        ms = jnp.mean(xx * xx, axis=-1, keepdims=True)
        dst_ref[pl.ds(dst_row0 + r0, rc), :] = (xx * lax.rsqrt(ms + EPS) * g).astype(dst_ref.dtype)
        return carry

    lax.fori_loop(0, n_rows // rc, body, 0)


def _in_proj_kernel(x_ref, head_ref, gain_ref, w_ref, cs_ref, wt_ref, out_ref, tail_ref, xn_ref, *,
                    n_rms, n_tail, tail_nt, tiles_per_batch, rc):
    j = pl.program_id(1)
    tm = x_ref.shape[0]
    tn = w_ref.shape[1]

    def normalise_with_tail(sources):
        g = gain_ref[...]
        for dst, (src_ref, src) in enumerate(sources):
            xx = src_ref[src * rc:(src + 1) * rc, :]
            ms = jnp.mean(xx * xx, axis=-1, keepdims=True)
            xn = (xx * lax.rsqrt(ms + EPS) * g).astype(xn_ref.dtype)
            xn_ref[dst * rc:(dst + 1) * rc, :] = xn
            tail = jnp.dot(xn, wt_ref[...], preferred_element_type=F32)
            if tail_nt:
                tail_ref[:, dst * rc:(dst + 1) * rc] = tail.T[0:n_tail]
            else:
                tail_ref[dst * rc:(dst + 1) * rc, :] = tail.astype(tail_ref.dtype)

    @pl.when(j == 0)
    def _():
        first = pl.program_id(0) % tiles_per_batch == 0
        n_head, n_all = BLK // rc, tm // rc

        @pl.when(first)
        def _():
            normalise_with_tail([(head_ref, r) for r in range(n_head)]
                                + [(x_ref, r) for r in range(n_all - n_head)])

        @pl.when(jnp.logical_not(first))
        def _():
            normalise_with_tail([(x_ref, r) for r in range(n_all)])

    @pl.when(j > 0)
    def _():
        acc = jnp.dot(xn_ref[...], w_ref[...], preferred_element_type=F32)
        cs = cs_ref[...]
        if n_rms > 0:
            @pl.when(j <= n_rms)
            def _():
                for hh in range(tn // LANES):
                    sl = slice(hh * LANES, (hh + 1) * LANES)
                    a = acc[:, sl]
                    ms = jnp.mean(a * a, axis=-1, keepdims=True)
                    out_ref[:, sl] = (a * lax.rsqrt(ms + EPS) * cs[:, sl]).astype(out_ref.dtype)

            @pl.when(j > n_rms)
            def _():
                out_ref[...] = (acc * cs).astype(out_ref.dtype)
        else:
            out_ref[...] = (acc * cs).astype(out_ref.dtype)


def _stream_window_spec(tp, tm, d):
    tiles_per_batch = tp // tm
    align = math.gcd(tm, BLK)

    def index_map(i, *_):
        start = jnp.maximum((i % tiles_per_batch) * tm - BLK, 0)
        return i // tiles_per_batch, pl.multiple_of(start, align), 0

    return pl.BlockSpec((None, pl.Element(tm), pl.Element(d)), index_map)


def _head_spec(head, tp, tm):
    tiles_per_batch = tp // tm
    per_batch = head.shape[0] > 1
    return pl.BlockSpec((None, BLK, head.shape[2]),
                        lambda i, *_: (i // tiles_per_batch if per_batch else 0, 0, 0))


def _in_proj(h, head, gain, w, colscale, *, n_rms, n_tail, tail_nt, tm, tn, tp):
    d = h.shape[2]
    n = h.shape[0] * tp
    width_main = colscale.shape[1]
    n_main = width_main // tn
    assert w.shape[2] == width_main + LANES and 0 < n_tail < LANES
    wt_spec = pl.BlockSpec((None, d, LANES), lambda i, j: (0, 0, width_main // LANES))
    if tail_nt:
        tail_shape = jax.ShapeDtypeStruct((n_tail, n), F32)
        tail_spec = pl.BlockSpec((n_tail, tm), lambda i, j: (0, i))
    else:
        tail_shape = jax.ShapeDtypeStruct((n, LANES), BF16)
        tail_spec = pl.BlockSpec((tm, LANES), lambda i, j: (i, 0))
    col = lambda i, j: (0, jnp.maximum(j - 1, 0))
    wcol = lambda i, j: (0, 0, jnp.maximum(j - 1, 0))
    kern = functools.partial(_in_proj_kernel, n_rms=n_rms, n_tail=n_tail, tail_nt=tail_nt,
                             tiles_per_batch=tp // tm, rc=_pick_tile(math.gcd(tm, BLK), RMS_ROWS))
    return pl.pallas_call(
        kern,
        grid=(n // tm, n_main + 1),
        in_specs=[
            _stream_window_spec(tp, tm, d),
            _head_spec(head, tp, tm),
            pl.BlockSpec((1, d), lambda i, j: (0, 0)),
            pl.BlockSpec((None, d, tn), wcol),
            pl.BlockSpec((1, tn), col),
            wt_spec,
        ],
        out_specs=[
            pl.BlockSpec((tm, tn), lambda i, j: (i, jnp.maximum(j - 1, 0))),
            tail_spec,
        ],
        out_shape=[jax.ShapeDtypeStruct((n, width_main), BF16), tail_shape],
        scratch_shapes=[pltpu.VMEM((tm, d), BF16)],
        compiler_params=_params(("parallel", "arbitrary")),
    )(h, head, gain, w, colscale, w)


def _mlstm_gates_kernel(graw_ref, bias_ref, gr_ref, gc_ref, *, tiles_per_batch):
    width = graw_ref.shape[1]
    g = graw_ref[...] + bias_ref[...]
    row = lax.broadcasted_iota(jnp.int32, g.shape, 0)
    pos = lax.broadcasted_iota(jnp.int32, g.shape, 1)
    lane = pos & (BLK - 1)
    is_forget = (row & ML_HEADS) != 0
    is_bwd = row >= 2 * ML_HEADS
    valid = (pos + (pl.program_id(0) % tiles_per_batch) * width) >= PAD
    lf = jnp.where(valid, _log_sigmoid(g), 0.0)
    pre = lf
    suf = lf
    d = 1
    while d < BLK:
        pre = pre + jnp.where(lane >= d, pltpu.roll(pre, d, axis=1), 0.0)
        suf = suf + jnp.where(lane < BLK - d, pltpu.roll(suf, width - d, axis=1), 0.0)
        d *= 2
    ig = jnp.where(valid, g, -jnp.inf)
    out = jnp.where(is_forget, jnp.where(is_bwd, suf, pre), ig)
    a = out - pltpu.roll(out, N_ML_GATES - ML_HEADS, axis=0)
    pmax = a
    smax = a
    d = 1
    while d < BLK:
        pmax = jnp.maximum(pmax, jnp.where(lane >= d, pltpu.roll(pmax, d, axis=1), -jnp.inf))
        smax = jnp.maximum(smax, jnp.where(lane < BLK - d, pltpu.roll(smax, width - d, axis=1), -jnp.inf))
        d *= 2
    runmax = jnp.where(is_forget, 0.0, jnp.where(is_bwd, smax, pmax))
    both = jnp.concatenate([out, runmax], axis=0)
    gr_ref[...] = both
    full = jnp.concatenate([both, jnp.zeros((LANES - 2 * N_ML_GATES, width), F32)], axis=0)
    gc_ref[...] = full.T


def _mlstm_gates(graw, bias, tp, width):
    n = graw.shape[1]
    return pl.pallas_call(
        functools.partial(_mlstm_gates_kernel, tiles_per_batch=tp // width),
        grid=(n // width,),
        in_specs=[pl.BlockSpec((N_ML_GATES, width), lambda i: (0, i)),
                  pl.BlockSpec((N_ML_GATES, 1), lambda i: (0, 0))],
        out_specs=[pl.BlockSpec((2 * N_ML_GATES, width), lambda i: (0, i)),
                   pl.BlockSpec((width, LANES), lambda i: (i, 0))],
        out_shape=[jax.ShapeDtypeStruct((2 * N_ML_GATES, n), F32),
                   jax.ShapeDtypeStruct((n, LANES), F32)],
        compiler_params=_params(("parallel",)),
    )(graw, bias)


def _na_bias_tables(rel_bias, rows):
    a = np.arange(NA_GROUP)[:, None]
    u = np.arange(NA_UNION)[None, :]
    half = WIN_H // 2
    vis_first = (u < WIN_H) & (a >= 0)
    dr_first = u - a + WIN_H - 1
    vis_mid = (u >= a) & (u < a + WIN_H)
    dr_mid = u - a + WIN_H - 1 - half
    lo = NA_UNION - WIN_H
    vis_last = (u >= lo) & (a >= 0)
    dr_last = u - a + (WIN_H - 1) - (NA_UNION - NA_GROUP)
    vis_r = np.stack([vis_first, vis_mid, vis_last])
    dr = np.clip(np.stack([dr_first, dr_mid, dr_last]), 0, 2 * WIN_H - 2)
    qc = np.arange(GRID_W)[:, None]
    kc = np.arange(GRID_W)[None, :]
    c0 = np.clip(qc - WIN_W // 2, 0, GRID_W - WIN_W)
    vis_c = (kc >= c0) & (kc < c0 + WIN_W)
    dc = np.clip(kc - qc + WIN_W - 1, 0, 2 * WIN_W - 2)
    sel_c = (dc[..., None] == np.arange(2 * WIN_W - 1)).astype(np.float32)
    cols = jnp.einsum('hrd,qkd->hrqk', rel_bias.astype(F32), sel_c, precision=lax.Precision.HIGHEST)
    cols = jnp.where(vis_c, cols, -jnp.inf)
    hidden = jnp.full((NA_HEADS, GRID_W, GRID_W), -jnp.inf, F32)
    return jnp.stack([
        jnp.concatenate([
            jnp.concatenate([cols[:, dr[c, qa, ku]] if vis_r[c, qa, ku] else hidden
                             for ku in range(NA_UNION)], axis=2)
            for qa in range(NA_GROUP)], axis=1)
        for c in range(3)])


def _na_kernel(q_ref, k_ref, v_ref, bias_ref, out_ref, *, rows):
    n_groups = rows // NA_GROUP
    gq = NA_GROUP * GRID_W
    uk = NA_UNION * GRID_W
    scale = NA_DIM ** -0.5
    km = k_ref[PAD:BLK, :]
    vm = v_ref[PAD:BLK, :]

    def attend(q, kw, vw, bias):
        s_m = lax.dot_general(q, km, _NT, preferred_element_type=F32) * scale
        m = jnp.max(s_m, axis=-1, keepdims=True)
        if kw is not None:
            s = lax.dot_general(q, kw, _NT, preferred_element_type=F32) * scale + bias
            m = jnp.maximum(m, jnp.max(s, axis=-1, keepdims=True))
        p_m = jnp.exp(s_m - m)
        den = jnp.sum(p_m, axis=-1, keepdims=True)
        y = jnp.dot(p_m.astype(BF16), vm, preferred_element_type=F32)
        if kw is not None:
            p = jnp.exp(s - m)
            den = den + jnp.sum(p, axis=-1, keepdims=True)
            y = y + jnp.dot(p.astype(BF16), vw, preferred_element_type=F32)
        return y / den

    out_ref[0:PAD, :] = jnp.zeros((PAD, NA_DIM), out_ref.dtype)
    out_ref[PAD:BLK, :] = attend(q_ref[PAD:BLK, :], None, None, None).astype(out_ref.dtype)

    def group(g):
        q0 = pl.multiple_of(BLK + g * gq, gq)
        u0 = jnp.clip(g * NA_GROUP - WIN_H // 2, 0, rows - NA_UNION)
        k0 = pl.multiple_of(BLK + u0 * GRID_W, GRID_W)
        cls = jnp.where(g == 0, 0, jnp.where(g == n_groups - 1, 2, 1))
        return attend(q_ref[pl.ds(q0, gq), :], k_ref[pl.ds(k0, uk), :], v_ref[pl.ds(k0, uk), :],
                      bias_ref[cls])

    def body(gg, carry):
        g = gg * NA_UNROLL
        y = jnp.concatenate([group(g + u) for u in range(NA_UNROLL)], axis=0)
        out_ref[pl.ds(pl.multiple_of(BLK + g * gq, gq), NA_UNROLL * gq), :] = y.astype(out_ref.dtype)
        return carry

    lax.fori_loop(0, n_groups // NA_UNROLL, body, 0)


def _na_attention(proj, bias, batch, tp):
    rows = (tp - BLK) // GRID_W
    n = proj.shape[0]
    gq = NA_GROUP * GRID_W
    uk = NA_UNION * GRID_W
    return pl.pallas_call(
        functools.partial(_na_kernel, rows=rows),
        grid=(batch, NA_HEADS),
        in_specs=[
            pl.BlockSpec((tp, NA_DIM), lambda b, h: (b, h)),
            pl.BlockSpec((tp, NA_DIM), lambda b, h: (b, NA_HEADS + h)),
            pl.BlockSpec((tp, NA_DIM), lambda b, h: (b, 2 * NA_HEADS + h)),
            pl.BlockSpec((3, None, gq, uk), lambda b, h: (0, h, 0, 0)),
        ],
        out_specs=pl.BlockSpec((tp, NA_DIM), lambda b, h: (b, h)),
        out_shape=jax.ShapeDtypeStruct((n, NA_WIDTH), BF16),
        compiler_params=_params(("parallel", "parallel")),
    )(proj, proj, proj, bias)


def _mlstm_dir(q_ref, k_ref, v_ref, gr_ref, gc_ref, y_ref, c_ref, m_ref, head, rev):
    q = q_ref[:, head * ML_QK:(head + 1) * ML_QK]
    k = k_ref[:, head * ML_QK:(head + 1) * ML_QK]
    v = v_ref[:, head * ML_V:(head + 1) * ML_V]
    ig = (2 * ML_HEADS if rev else 0) + head
    b = ig + ML_HEADS
    run = N_ML_GATES + ig
    a_r = gr_ref[ig:ig + 1, :] - gr_ref[b:b + 1, :]
    a_c = gc_ref[:, ig:ig + 1] - gc_ref[:, b:b + 1]
    b_c = gc_ref[:, b:b + 1]
    run_c = gc_ref[:, run:run + 1]
    m = m_ref[head]
    m_rel = jnp.broadcast_to(jnp.maximum(m, run_c), (BLK, LANES))
    ti = lax.broadcasted_iota(jnp.int32, (BLK, BLK), 0)
    si = lax.broadcasted_iota(jnp.int32, (BLK, BLK), 1)
    mask = (si >= ti) if rev else (si <= ti)
    arg = jnp.where(mask, a_r - jnp.concatenate([m_rel] * (BLK // LANES), axis=1), -jnp.inf)
    p = jnp.exp(arg) * lax.dot_general(q, k, _NT, preferred_element_type=F32)
    w_prev = jnp.exp(m - m_rel)
    v_ext = jnp.concatenate([v, jnp.ones((BLK, LANES), BF16)], axis=1)
    c_ext = c_ref[head]
    num = (jnp.concatenate([w_prev] * (c_ext.shape[1] // LANES), axis=1)
           * jnp.dot(q, c_ext.astype(BF16), preferred_element_type=F32)
           + jnp.dot(p.astype(BF16), v_ext, preferred_element_type=F32))
    den = num[:, ML_V:]
    inv = 1.0 / jnp.maximum(jnp.abs(den), jnp.exp(-(jnp.broadcast_to(b_c, (BLK, LANES)) + m_rel)))
    y_ref[:, head * ML_V:(head + 1) * ML_V] = (
        num[:, :ML_V] * jnp.concatenate([inv] * (ML_V // LANES), axis=1)).astype(y_ref.dtype)
    last = 0 if rev else BLK - 1
    b_end = gr_ref[b:b + 1, last:last + 1]
    m_end = jnp.maximum(m, gr_ref[run:run + 1, last:last + 1])
    w_tok = jnp.exp(a_c - m_end)
    c_ref[head] = jnp.exp(m - m_end) * c_ext + lax.dot_general(
        k, (w_tok * v_ext.astype(F32)).astype(BF16), _TN, preferred_element_type=F32)
    m_ref[head] = b_end + m_end


def _mlstm_kernel(qf, kf, vf, grf, gcf, qb, kb, vb, grb, gcb, yf, yb, cf, mf, cb, mb):
    @pl.when(pl.program_id(1) == 0)
    def _():
        cf[...] = jnp.zeros(cf.shape, F32)
        cb[...] = jnp.zeros(cb.shape, F32)
        mf[...] = jnp.zeros(mf.shape, F32)
        mb[...] = jnp.zeros(mb.shape, F32)

    for head in range(ML_HEADS):
        _mlstm_dir(qf, kf, vf, grf, gcf, yf, cf, mf, head, False)
        _mlstm_dir(qb, kb, vb, grb, gcb, yb, cb, mb, head, True)


def _mlstm(proj, gr, gc, batch, nb):
    n = proj.shape[0]
    q0 = 3 * NA_WIDTH // ML_QK_WIDTH
    k0 = q0 + 1
    v0 = (3 * NA_WIDTH + 2 * ML_QK_WIDTH) // ML_WIDTH
    fwd = lambda b, s: b * nb + s
    bwd = lambda b, s: b * nb + nb - 1 - s

    def specs(pos):
        return [
            pl.BlockSpec((BLK, ML_QK_WIDTH), lambda b, s: (pos(b, s), q0)),
            pl.BlockSpec((BLK, ML_QK_WIDTH), lambda b, s: (pos(b, s), k0)),
            pl.BlockSpec((BLK, ML_WIDTH), lambda b, s: (pos(b, s), v0)),
            pl.BlockSpec((2 * N_ML_GATES, BLK), lambda b, s: (0, pos(b, s))),
            pl.BlockSpec((BLK, LANES), lambda b, s: (pos(b, s), 0)),
        ]

    state = [pltpu.VMEM((ML_HEADS, ML_QK, ML_V + LANES), F32), pltpu.VMEM((ML_HEADS, 1, 1), F32)]
    return pl.pallas_call(
        _mlstm_kernel,
        grid=(batch, nb),
        in_specs=specs(fwd) + specs(bwd),
        out_specs=[pl.BlockSpec((BLK, ML_WIDTH), lambda b, s: (fwd(b, s), 0)),
                   pl.BlockSpec((BLK, ML_WIDTH), lambda b, s: (bwd(b, s), 0))],
        out_shape=[jax.ShapeDtypeStruct((n, ML_WIDTH), BF16)] * 2,
        scratch_shapes=state + state,
        compiler_params=_params(("parallel", "arbitrary")),
    )(proj, proj, proj, gr, gc, proj, proj, proj, gr, gc)


def _gla_dir(q_ref, k_ref, v_ref, lr_ref, gu_ref, gb_ref, tri_ref, o_ref, st_ref, valid, head, rev):
    dk = GLA_DK
    ksl = slice(head * GLA_DK, (head + 1) * GLA_DK)
    vsl = slice(head * GLA_DV, (head + 1) * GLA_DV)
    z = jnp.dot(lr_ref[...], gu_ref[head], preferred_element_type=F32) + gb_ref[head]
    la = jnp.where(valid, _log_sigmoid(z) * (1.0 / GLA_TAU), 0.0)
    t1 = la.astype(BF16)
    t2 = (la - t1.astype(F32)).astype(BF16)
    tri = tri_ref[...]
    b = jnp.dot(tri, t1, preferred_element_type=F32) + jnp.dot(tri, t2, preferred_element_type=F32)
    q = q_ref[:, ksl].astype(F32)
    k = k_ref[:, ksl].astype(F32)
    b_end = b[0:1] if rev else b[BLK - 1:BLK]
    q_in = (q * jnp.exp(b)).astype(BF16)
    k_dec = (k * jnp.exp(b_end - b)).astype(BF16)
    sub = GLA_CHUNK
    ti = lax.broadcasted_iota(jnp.int32, (sub, BLK), 0)
    si = lax.broadcasted_iota(jnp.int32, (sub, BLK), 1)

    def at_rows(x, row0):
        parts = [jnp.zeros((n, dk), BF16) for n in (row0,) if n] + [x]
        after = BLK - row0 - x.shape[0]
        if after:
            parts.append(jnp.zeros((after, dk), BF16))
        return jnp.concatenate(parts, axis=0) if len(parts) > 1 else x

    a_rows = []
    for i in range(BLK // sub):
        lo, hi = i * sub, (i + 1) * sub
        b_i, q_i = b[lo:hi], q[lo:hi]
        mu = b[lo + sub // 2:lo + sub // 2 + 1]
        lhs = (q_i * jnp.exp(b_i - mu)).astype(BF16)
        rhs = at_rows((k[lo:hi] * jnp.exp(mu - b_i)).astype(BF16), lo)
        n_prev = BLK - hi if rev else lo
        if n_prev:
            prev = slice(hi, BLK) if rev else slice(0, lo)
            rho = b[hi:hi + 1] if rev else b[lo - 1:lo]
            k_prev = at_rows((k[prev] * jnp.exp(rho - b[prev])).astype(BF16), hi if rev else 0)
            lhs = jnp.concatenate([(q_i * jnp.exp(b_i - rho)).astype(BF16), lhs], axis=1)
            rhs = jnp.concatenate([k_prev, rhs], axis=1)
        a_i = lax.dot_general(lhs, rhs, _NT, preferred_element_type=F32)
        seen = (si >= ti + lo) if rev else (si <= ti + lo)
        a_rows.append(jnp.where(seen, a_i, 0.0).astype(BF16))
    a = jnp.concatenate(a_rows, axis=0)
    st = st_ref[head]
    v = v_ref[:, vsl]
    o_ref[:, vsl] = (lax.dot_general(q_in, st.astype(BF16), _NT, preferred_element_type=F32)
                     + jnp.dot(a, v, preferred_element_type=F32)).astype(o_ref.dtype)
    st_ref[head] = st * jnp.exp(b_end) + lax.dot_general(v, k_dec, _TN, preferred_element_type=F32)


def _gla_kernel(qf, kf, vf, lrf, guf, gbf, bdf, qb, kb, vb, lrb, gub, gbb, bdb, of, ob, sf, sb, *, nb):
    s = pl.program_id(2)

    @pl.when(s == 0)
    def _():
        sf[...] = jnp.zeros(sf.shape, F32)
        sb[...] = jnp.zeros(sb.shape, F32)

    row = lax.broadcasted_iota(jnp.int32, (BLK, 1), 0)
    for head in range(GLA_GROUP):
        _gla_dir(qf, kf, vf, lrf, guf, gbf, bdf, of, sf, (row + s * BLK) >= PAD, head, False)
        _gla_dir(qb, kb, vb, lrb, gub, gbb, bdb, ob, sb, (row + (nb - 1 - s) * BLK) >= PAD, head, True)


def _gla(proj, lr, gate_up, gate_bias, batch, nb):
    n = proj.shape[0]
    grp = GLA_GROUP
    k0 = GLA_KW // (grp * GLA_DK)
    v0 = 2 * GLA_KW // (grp * GLA_DV)
    fwd = lambda b, h, s: b * nb + s
    bwd = lambda b, h, s: b * nb + nb - 1 - s
    t = np.arange(BLK)
    bd_f = jnp.asarray(t[None, :] <= t[:, None], BF16)
    bd_b = jnp.asarray(t[None, :] >= t[:, None], BF16)

    def specs(pos, direction):
        return [
            pl.BlockSpec((BLK, grp * GLA_DK), lambda b, h, s: (pos(b, h, s), h)),
            pl.BlockSpec((BLK, grp * GLA_DK), lambda b, h, s: (pos(b, h, s), k0 + h)),
            pl.BlockSpec((BLK, grp * GLA_DV), lambda b, h, s: (pos(b, h, s), v0 + h)),
            pl.BlockSpec((BLK, LANES), lambda b, h, s: (pos(b, h, s), 0)),
            pl.BlockSpec((None, grp, LANES, GLA_DK), lambda b, h, s: (direction, h, 0, 0)),
            pl.BlockSpec((None, grp, 1, GLA_DK), lambda b, h, s: (direction, h, 0, 0)),
            pl.BlockSpec((BLK, BLK), lambda b, h, s: (0, 0)),
        ]

    state = pltpu.VMEM((grp, GLA_DV, GLA_DK), F32)
    return pl.pallas_call(
        functools.partial(_gla_kernel, nb=nb),
        grid=(batch, GLA_HEADS // grp, nb),
        in_specs=specs(fwd, 0) + specs(bwd, 1),
        out_specs=[pl.BlockSpec((BLK, grp * GLA_DV), lambda b, h, s: (fwd(b, h, s), h)),
                   pl.BlockSpec((BLK, grp * GLA_DV), lambda b, h, s: (bwd(b, h, s), h))],
        out_shape=[jax.ShapeDtypeStruct((n, GLA_VW), BF16)] * 2,
        scratch_shapes=[state, state],
        compiler_params=_params(("parallel", "parallel", "arbitrary")),
    )(proj, proj, proj, lr, gate_up, gate_bias, bd_f, proj, proj, proj, lr, gate_up, gate_bias, bd_b)


def _gated_head_rms(yf_ref, yb_ref, gate_ref, gain_ref, n_heads, act):
    y = yf_ref[...].astype(F32) + yb_ref[...].astype(F32)
    gate = act(gate_ref[...].astype(F32))
    gain = gain_ref[...]
    hd = y.shape[1] // n_heads
    outs = []
    for hh in range(n_heads):
        sl = slice(hh * hd, (hh + 1) * hd)
        a = y[:, sl]
        ms = jnp.mean(a * a, axis=-1, keepdims=True)
        outs.append((a * lax.rsqrt(ms + EPS) * gain[:, sl] * gate[:, sl]).astype(BF16))
    return jnp.concatenate(outs, axis=1)


def _add_residual(out_ref, x_ref, head_ref, mix, tiles_per_batch):
    tm = out_ref.shape[0]
    first = pl.program_id(0) % tiles_per_batch == 0

    @pl.when(first)
    def _():
        row = lax.broadcasted_iota(jnp.int32, (BLK, 1), 0)
        out_ref[0:BLK, :] = jnp.where(row >= PAD, head_ref[...] + mix[0:BLK], 0.0)
        if tm > BLK:
            out_ref[BLK:, :] = x_ref[0:tm - BLK, :] + mix[BLK:]

    @pl.when(jnp.logical_not(first))
    def _():
        out_ref[...] = x_ref[...] + mix


def _out_proj0_kernel(ya_ref, yf_ref, yb_ref, gate_ref, gain_ref, wa_ref, wb_ref, x_ref, head_ref,
                      out_ref, *, tiles_per_batch):
    y_ml = _gated_head_rms(yf_ref, yb_ref, gate_ref, gain_ref, ML_HEADS, _sigmoid)
    mix = (jnp.dot(ya_ref[...], wa_ref[...], preferred_element_type=F32)
           + jnp.dot(y_ml, wb_ref[...], preferred_element_type=F32))
    _add_residual(out_ref, x_ref, head_ref, mix, tiles_per_batch)


def _out_proj1_kernel(yf_ref, yb_ref, gate_ref, gain_ref, w_ref, x_ref, head_ref, out_ref, *,
                      tiles_per_batch):
    y = _gated_head_rms(yf_ref, yb_ref, gate_ref, gain_ref, GLA_HEADS, lambda r: r * _sigmoid(r))
    mix = jnp.dot(y, w_ref[...], preferred_element_type=F32)
    _add_residual(out_ref, x_ref, head_ref, mix, tiles_per_batch)


def _out_proj0(y_na, y_f, y_b, proj, gain, w, x, head, tp, tm):
    d = x.shape[2]
    n = x.shape[0] * tp
    assert NA_WIDTH == ML_WIDTH
    gate_blk = (3 * NA_WIDTH + 2 * ML_QK_WIDTH + ML_WIDTH) // ML_WIDTH
    row = lambda i: (i, 0)
    const = lambda i: (0, 0)
    return pl.pallas_call(
        functools.partial(_out_proj0_kernel, tiles_per_batch=tp // tm),
        grid=(n // tm,),
        in_specs=[
            pl.BlockSpec((tm, NA_WIDTH), row),
            pl.BlockSpec((tm, ML_WIDTH), row),
            pl.BlockSpec((tm, ML_WIDTH), row),
            pl.BlockSpec((tm, ML_WIDTH), lambda i: (i, gate_blk)),
            pl.BlockSpec((1, ML_WIDTH), const),
            pl.BlockSpec((NA_WIDTH, d), const, pipeline_mode=pl.Buffered(1)),
            pl.BlockSpec((ML_WIDTH, d), lambda i: (1, 0), pipeline_mode=pl.Buffered(1)),
            _stream_window_spec(tp, tm, d),
            _head_spec(head, tp, tm),
        ],
        out_specs=pl.BlockSpec((tm, d), row),
        out_shape=jax.ShapeDtypeStruct((n, d), F32),
        compiler_params=_params(("parallel",)),
    )(y_na, y_f, y_b, proj, gain, w, w, x, head)


def _out_proj1(o_f, o_b, proj, gain, w, x, head, tp, tm):
    d = x.shape[2]
    n = x.shape[0] * tp
    gate_blk = (2 * GLA_KW + GLA_VW) // GLA_VW
    row = lambda i: (i, 0)
    const = lambda i: (0, 0)
    return pl.pallas_call(
        functools.partial(_out_proj1_kernel, tiles_per_batch=tp // tm),
        grid=(n // tm,),
        in_specs=[
            pl.BlockSpec((tm, GLA_VW), row),
            pl.BlockSpec((tm, GLA_VW), row),
            pl.BlockSpec((tm, GLA_VW), lambda i: (i, gate_blk)),
            pl.BlockSpec((1, GLA_VW), const),
            pl.BlockSpec((GLA_VW, d), const, pipeline_mode=pl.Buffered(1)),
            _stream_window_spec(tp, tm, d),
            _head_spec(head, tp, tm),
        ],
        out_specs=pl.BlockSpec((tm, d), row),
        out_shape=jax.ShapeDtypeStruct((n, d), F32),
        compiler_params=_params(("parallel",)),
    )(o_f, o_b, proj, gain, w, x, head)


def _ffn_kernel(h_ref, hp_ref, hn_ref, gain_ref, wg_ref, wv_ref, cpg_ref, cpv_ref, wd_ref, out_ref,
                xn_ref, *, n_sub, meta_blocks, rc):
    i = pl.program_id(0)
    c = pl.program_id(1)
    tm = h_ref.shape[0]

    def prologue():
        g = gain_ref[...]

        def norm(xx):
            ms = jnp.mean(xx * xx, axis=-1, keepdims=True)
            return xx * lax.rsqrt(ms + EPS) * g

        xn_ref[0:HALO, :] = norm(hp_ref[...]).astype(xn_ref.dtype)
        for r in range(tm // rc):
            rows = slice(r * rc, (r + 1) * rc)
            xx = h_ref[rows, :]
            xn_ref[HALO + r * rc:HALO + (r + 1) * rc, :] = norm(xx).astype(xn_ref.dtype)
            out_ref[rows, :] = xx
        nxt = norm(hn_ref[...])
        if not meta_blocks:
            nxt = nxt * (i != pl.num_programs(0) - 1).astype(F32)
        xn_ref[HALO + tm:, :] = nxt.astype(xn_ref.dtype)

    def conv(row0, rows, w_ref, cp_ref):
        u = jnp.dot(xn_ref[row0:row0 + rows + 2 * HALO, :], w_ref[...],
                    preferred_element_type=F32)
        cp = cp_ref[...]
        out = cp[CONV_W:CONV_W + 1]
        for tap in range(CONV_W):
            off = HALO - CONV_W // 2 + tap
            out = out + u[off:off + rows] * cp[tap:tap + 1]
        return out

    def chunk():
        rows = tm // n_sub
        acts = []
        for s in range(n_sub):
            g = conv(s * rows, rows, wg_ref, cpg_ref)
            val = conv(s * rows, rows, wv_ref, cpv_ref)
            acts.append((g * _sigmoid(g) * val).astype(BF16))
        for s in range(n_sub):
            sl = slice(s * rows, (s + 1) * rows)
            out_ref[sl, :] += jnp.dot(acts[s], wd_ref[...], preferred_element_type=F32)

    @pl.when(c == 0)
    def _():
        prologue()
        chunk()

    @pl.when(c > 0)
    def _():
        chunk()

    if meta_blocks:
        @pl.when(c == pl.num_programs(1) - 1)
        def _():
            out_ref[0:PAD, :] = jnp.zeros((PAD, out_ref.shape[1]), F32)


def _ffn(h, layer, gain, w_up, conv_p, w_down, tp, tm, tf, meta_blocks):
    n, d = h.shape
    d_ff = w_down.shape[1]
    n_ff = d_ff // tf
    if meta_blocks:
        assert tm == BLK
        n_out = n // tp * BLK
        row0 = lambda i: i * tp
    else:
        tiles_per_batch = (tp - BLK) // tm
        n_out = n // tp * (tp - BLK)
        row0 = lambda i: (i // tiles_per_batch) * tp + BLK + (i % tiles_per_batch) * tm
    window = lambda rows, start: pl.BlockSpec((pl.Element(rows), pl.Element(d)),
                                              lambda i, c: (pl.multiple_of(start(i), HALO), 0))
    h_specs = [window(tm, row0), window(HALO, lambda i: jnp.maximum(row0(i) - HALO, 0)),
               window(HALO, lambda i: jnp.minimum(row0(i) + tm, n - HALO))]
    return pl.pallas_call(
        functools.partial(_ffn_kernel, n_sub=FFN_SUBTILES, meta_blocks=meta_blocks,
                          rc=_pick_tile(tm, RMS_ROWS)),
        grid=(n_out // tm, n_ff),
        in_specs=h_specs + [
            pl.BlockSpec((None, 1, d), lambda i, c: (layer, 0, 0)),
            pl.BlockSpec((None, d, tf), lambda i, c: (layer, 0, c)),
            pl.BlockSpec((None, d, tf), lambda i, c: (layer, 0, n_ff + c)),
            pl.BlockSpec((None, conv_p.shape[1], tf), lambda i, c: (layer, 0, c)),
            pl.BlockSpec((None, conv_p.shape[1], tf), lambda i, c: (layer, 0, n_ff + c)),
            pl.BlockSpec((None, tf, d), lambda i, c: (layer, c, 0)),
        ],
        out_specs=pl.BlockSpec((tm, d), lambda i, c: (i, 0)),
        out_shape=jax.ShapeDtypeStruct((n_out, d), F32),
        scratch_shapes=[pltpu.VMEM((tm + 2 * HALO, d), BF16)],
        compiler_params=_params(("parallel", "arbitrary")),
    )(h, h, h, gain, w_up, w_up, conv_p, conv_p, w_down)


def kernel(x, meta_tokens, norm_mix, norm_ffn, ab_w_in, ab_gate_bias, ab_q_gain, ab_k_gain, ab_rel_bias, ab_ml_gain, ab_w_out, c_w_in, c_gate_up, c_gate_bias, c_head_gain, c_w_out, ffn_w_up, ffn_conv_w, ffn_conv_b, ffn_w_down):
    batch, seq, d = x.shape
    d_ff = ffn_w_down.shape[1]
    assert seq % (NA_UNROLL * NA_GROUP * GRID_W) == 0 and seq // GRID_W >= NA_UNION and d % LANES == 0
    tp = BLK + seq
    nb = tp // BLK
    tm_proj = _pick_tile(tp, (1280, 1024, 768, 512, 256))
    tm_out = _pick_tile(tp, (640, 512, 256))
    tm_ffn = _pick_tile(seq, (1024, 512, 256))
    tn_proj = 1024
    tf = _pick_tile(d_ff, (512, 256, 128))

    head = jnp.concatenate([jnp.zeros((PAD, d), x.dtype), meta_tokens.astype(x.dtype)], axis=0)[None]

    ones = lambda k: jnp.ones((k,), F32)
    colscale0 = jnp.concatenate([
        jnp.tile(ab_q_gain[0].astype(F32), NA_HEADS), jnp.tile(ab_k_gain[0].astype(F32), NA_HEADS),
        ones(NA_WIDTH), ones(ML_QK_WIDTH), jnp.full((ML_QK_WIDTH,), ML_QK ** -0.5, F32),
        ones(2 * ML_WIDTH)])[None]
    lane_pad = lambda w, n_tail: jnp.pad(w, ((0, 0), (0, 0), (0, LANES - n_tail))).astype(BF16)
    proj0, graw = _in_proj(x, head, norm_mix[0][None].astype(F32), lane_pad(ab_w_in, N_ML_GATES), colscale0,
                           n_rms=2 * NA_WIDTH // tn_proj, n_tail=N_ML_GATES, tail_nt=True,
                           tm=tm_proj, tn=tn_proj, tp=tp)
    gr, gc = _mlstm_gates(graw, ab_gate_bias[0].astype(F32)[:, None], tp, tm_proj)
    y_na = _na_attention(proj0, _na_bias_tables(ab_rel_bias[0], seq // GRID_W), batch, tp)
    y_f, y_b = _mlstm(proj0, gr, gc, batch, nb)
    h = _out_proj0(y_na, y_f, y_b, proj0, ab_ml_gain[0][None].astype(F32), ab_w_out[0].astype(BF16),
                   x, head, tp, tm_out)
    conv_p = jnp.concatenate([ffn_conv_w.astype(F32), ffn_conv_b[:, None].astype(F32),
                              jnp.zeros((ffn_conv_w.shape[0], SUBLANES - CONV_W - 1, 2 * d_ff), F32)], axis=1)
    ffn_params = (norm_ffn[:, None].astype(F32), ffn_w_up.astype(BF16), conv_p, ffn_w_down.astype(BF16))
    x1 = _ffn(h, 0, *ffn_params, tp, tm_ffn, tf, False).reshape(batch, seq, d)
    head1 = _ffn(h, 0, *ffn_params, tp, BLK, tf, True).reshape(batch, BLK, d)

    n_main1 = 2 * GLA_KW + 2 * GLA_VW
    colscale1 = jnp.concatenate([jnp.full((GLA_KW,), GLA_DK ** -0.5, F32), ones(n_main1 - GLA_KW)])[None]
    proj1, lr = _in_proj(x1, head1, norm_mix[1][None].astype(F32), lane_pad(c_w_in, 2 * GLA_RANK), colscale1,
                         n_rms=0, n_tail=2 * GLA_RANK, tail_nt=False, tm=tm_proj, tn=tn_proj, tp=tp)
    gu = c_gate_up[0].astype(BF16).reshape(2, GLA_RANK, GLA_HEADS, GLA_DK).transpose(0, 2, 1, 3)
    gate_up = jnp.stack([jnp.pad(gu[0], ((0, 0), (0, LANES - GLA_RANK), (0, 0))),
                         jnp.pad(gu[1], ((0, 0), (GLA_RANK, LANES - 2 * GLA_RANK), (0, 0)))])
    gate_bias = c_gate_bias[0].astype(F32).reshape(2, GLA_HEADS, 1, GLA_DK)
    o_f, o_b = _gla(proj1, lr, gate_up, gate_bias, batch, nb)
    h = _out_proj1(o_f, o_b, proj1, c_head_gain[0][None].astype(F32), c_w_out[0].astype(BF16),
                   x1, head1, tp, tm_out)
    return _ffn(h, 1, *ffn_params, tp, tm_ffn, tf, False).reshape(batch, seq, d)
```
